```python
import math
import jax, jax.numpy as jnp
from jax import lax
import numpy as np

D_MODEL = 1024
BATCH = 8
SEQ = 2048
DEPTH = 2
DEC_BATCH = 32
DEC_SEQ = 1
PAST_LEN = 8192
PAGE_SIZE = 128

N_EVEN = (DEPTH + 1) // 2
N_ODD = DEPTH // 2
N_MEM = 256
NORM_EPS = 1e-6
RW_HD = 64
RW_W = D_MODEL // 2
RW_HEADS = RW_W // RW_HD
LORA_W = 64
LORA_A = 64
LORA_G = 128
RW_COLS = 3 * RW_W + LORA_W + LORA_A + LORA_G
RW_SPLITS = [RW_W, 2 * RW_W, 3 * RW_W, 3 * RW_W + LORA_W, 3 * RW_W + LORA_W + LORA_A]
RW_LN_EPS = RW_HD * 1e-5
LRU_W = D_MODEL // 2
LRU_BLOCKS = 8
LRU_BD = LRU_W // LRU_BLOCKS
CONV_W = 4
LRU_C = 8.0
AB_COLS = RW_COLS + 2 * LRU_W
MIX_W = RW_W + LRU_W
SB_HD = 64
SB_W = D_MODEL
SB_HEADS = SB_W // SB_HD
Q_BLOCK = 128
SB_BIAS_LO = -8.0
SB_BIAS_HI = -5.0
MEM_HEADS = 4
MEM_W = D_MODEL // 2
MEM_HD = MEM_W // MEM_HEADS
D_FF = -(-8 * D_MODEL // (3 * 256)) * 256

kernel_name = "rwkv7_rglru_stickbreak_hybrid_step"


def rms_norm(x, g, eps=NORM_EPS):
    xf = x.astype(jnp.float32)
    y = xf * lax.rsqrt(jnp.mean(xf * xf, axis=-1, keepdims=True) + eps)
    return (y * g.astype(jnp.float32)).astype(x.dtype)


def swiglu(h, w_gate, w_up, w_down):
    return (jax.nn.silu(h @ w_gate) * (h @ w_up)) @ w_down


def memory_kv(mem, g, wk, wv, kn):
    b, m, _ = mem.shape
    mn = rms_norm(mem, g)
    k = rms_norm((mn @ wk).reshape(b, m, MEM_HEADS, MEM_HD), kn)
    v = (mn @ wv).reshape(b, m, MEM_HEADS, MEM_HD)
    return k, v


def memory_xattn(h, k, v, wq, qn, wo):
    b, t, _ = h.shape
    q = rms_norm((h @ wq).reshape(b, t, MEM_HEADS, MEM_HD), qn)
    s = jnp.einsum('bthd,bmhd->bhtm', q, k, preferred_element_type=jnp.float32) * (MEM_HD ** -0.5)
    p = jax.nn.softmax(s, axis=-1).astype(v.dtype)
    o = jnp.einsum('bhtm,bmhd->bthd', p, v).reshape(b, t, MEM_W)
    return o @ wo


def rwkv_lru_mixer(h, wkv0, shift0, h0, conv0, w_in, mu, w0, w2, a0, a2, g2, k_k, k_a, r_k,
                   lnx_w, lnx_b, conv_w, conv_b, wa, ba, wx, bx, lam, w_out):
    b, t, _ = h.shape
    f32 = jnp.float32
    p = (h @ w_in).astype(f32)
    p_rw, p_x, p_gate = jnp.split(p, [RW_COLS, RW_COLS + LRU_W], axis=-1)

    prev = jnp.concatenate([shift0.astype(f32)[:, None], p_rw[:, :-1]], axis=1)
    xs = p_rw + mu * (prev - p_rw)
    r, k, v, dw, da, dg = jnp.split(xs, RW_SPLITS, axis=-1)
    log_w = -jnp.exp(-jax.nn.softplus(-(w0 + jnp.tanh(dw) @ w2)) - 0.5)
    a = jax.nn.sigmoid(a0 + da @ a2)
    g = jax.nn.sigmoid(dg) @ g2
    heads = lambda z: z.reshape(b, t, RW_HEADS, RW_HD)
    kk = heads(k * k_k)
    kk = kk / jnp.maximum(jnp.linalg.norm(kk, axis=-1, keepdims=True), 1e-12)
    k = k * (1.0 + (a - 1.0) * k_a)
    r, k, v, a, w = heads(r), heads(k), heads(v), heads(a), heads(jnp.exp(log_w))

    def step(S, inp):
        r_t, w_t, k_t, v_t, kk_t, a_t = inp
        sa = jnp.einsum('bhvk,bhk->bhv', S, -kk_t)
        S = (S * w_t[:, :, None, :] + sa[..., None] * (kk_t * a_t)[:, :, None, :]
             + v_t[..., None] * k_t[:, :, None, :])
        return S, jnp.einsum('bhvk,bhk->bhv', S, r_t)

    tm = lambda z: jnp.moveaxis(z, 1, 0)
    wkv, y = lax.scan(step, wkv0.astype(f32), (tm(r), tm(w), tm(k), tm(v), tm(kk), tm(a)))
    y = jnp.moveaxis(y, 0, 1)
    mean = jnp.mean(y, axis=-1, keepdims=True)
    var = jnp.mean(jnp.square(y - mean), axis=-1, keepdims=True)
    y = ((y - mean) * lax.rsqrt(var + RW_LN_EPS)).reshape(b, t, RW_W) * lnx_w + lnx_b
    bonus = jnp.sum(r * k * r_k.reshape(RW_HEADS, RW_HD), axis=-1, keepdims=True) * v
    out_rw = (y + bonus.reshape(b, t, RW_W)) * g

    xin = jnp.concatenate([conv0.astype(f32), p_x], axis=1)
    xc = conv_b + sum(conv_w[j] * xin[:, j:j + t] for j in range(CONV_W))
    new_conv = xin[:, -(CONV_W - 1):]
    xb = xc.reshape(b, t, LRU_BLOCKS, LRU_BD)
    gate_r = jax.nn.sigmoid(jnp.einsum('btnd,nde->btne', xb, wa).reshape(b, t, LRU_W) + ba)
    gate_i = jax.nn.sigmoid(jnp.einsum('btnd,nde->btne', xb, wx).reshape(b, t, LRU_W) + bx)
    log_a = -LRU_C * gate_r * jax.nn.softplus(-lam)
    a_t = jnp.exp(log_a)
    b_t = jnp.sqrt(-jnp.expm1(2.0 * log_a)) * (gate_i * xc)
    b_t = b_t.at[:, 0].add(a_t[:, 0] * h0.astype(f32))
    _, hs = lax.associative_scan(lambda l, rr: (l[0] * rr[0], rr[0] * l[1] + rr[1]),
                                 (a_t, b_t), axis=1)
    out_lru = hs * jax.nn.gelu(p_gate)

    mix = jnp.concatenate([out_rw, out_lru], axis=-1).astype(h.dtype) @ w_out
    dt = h.dtype
    return mix, (wkv.astype(dt), p_rw[:, -1].astype(dt), hs[:, -1].astype(dt), new_conv.astype(dt))


def sb_block(q, k, v, q_pos, k_pos, bias):
    z = (jnp.einsum('bqhd,bkhd->bhqk', q, k, preferred_element_type=jnp.float32) * (SB_HD ** -0.5)
         + bias.astype(jnp.float32)[None, :, None, None])
    mask = k_pos[None, :] < q_pos[:, None]
    log_1m = jnp.where(mask, jax.nn.log_sigmoid(-z), 0.0)
    after = lax.cumsum(log_1m, axis=3, reverse=True) - log_1m
    att = jnp.where(mask, jnp.exp(jax.nn.log_sigmoid(z) + after), 0.0)
    return jnp.einsum('bhqk,bkhd->bqhd', att.astype(v.dtype), v)


def sb_sweep(q, k, v, offset, bias):
    t = q.shape[1]
    pos = jnp.arange(offset + t)
    outs = []
    for s in range(0, t, Q_BLOCK):
        e = min(s + Q_BLOCK, t)
        outs.append(sb_block(q[:, s:e], k[:, :offset + e], v[:, :offset + e],
                             pos[offset + s:offset + e], pos[:offset + e], bias))
    return jnp.concatenate(outs, axis=1)


def stick_breaking_mixer(h, w_qkv, w_out, bias, past):
    b, t, _ = h.shape
    qkv = (h @ w_qkv).reshape(b, t, 3, SB_HEADS, SB_HD)
    q, k, v = qkv[:, :, 0], qkv[:, :, 1], qkv[:, :, 2]
    if past is None:
        k_all, v_all, offset = k, v, 0
    else:
        k_pool, v_pool, layer, page_table = past
        offset = page_table.shape[1] * PAGE_SIZE
        k_past = k_pool[layer][page_table].reshape(b, offset, SB_HEADS, SB_HD)
        v_past = v_pool[layer][page_table].reshape(b, offset, SB_HEADS, SB_HD)
        k_all = jnp.concatenate([k_past, k], axis=1)
        v_all = jnp.concatenate([v_past, v], axis=1)
    o = sb_sweep(q, k_all, v_all, offset, bias)
    return o.reshape(b, t, SB_W) @ w_out, (k, v)


def setup_inputs(seed: int = 0) -> dict:
    key = jax.random.key(seed)
    keys = iter(jax.random.split(key, 80))

    def nrm(shape, scale=1.0):
        return jax.random.normal(next(keys), shape, jnp.float32) * scale

    def gain(shape):
        return 1.0 + nrm(shape, 0.05)

    n_pages = PAST_LEN // PAGE_SIZE
    used = DEC_BATCH * n_pages
    pool = used + max(1, used // 4)
    page_table = jax.random.permutation(next(keys), pool)[:used].reshape(DEC_BATCH, n_pages).astype(jnp.int32)
    u = jax.random.uniform(next(keys), (N_EVEN, LRU_W), jnp.float32, minval=0.9, maxval=0.999)
    return {
        "x_prompt": nrm((BATCH, SEQ, D_MODEL)),
        "x_sample": nrm((DEC_BATCH, DEC_SEQ, D_MODEL)),
        "state_rwkv_wkv": nrm((N_EVEN, DEC_BATCH, RW_HEADS, RW_HD, RW_HD), 0.1),
        "state_rwkv_shift": nrm((N_EVEN, DEC_BATCH, RW_COLS)),
        "state_lru_h": nrm((N_EVEN, DEC_BATCH, LRU_W), 0.5),
        "state_lru_conv": nrm((N_EVEN, DEC_BATCH, CONV_W - 1, LRU_W)),
        "cache_sb_k": nrm((N_ODD, pool, PAGE_SIZE, SB_HEADS, SB_HD)),
        "cache_sb_v": nrm((N_ODD, pool, PAGE_SIZE, SB_HEADS, SB_HD)),
        "page_table": page_table,
        "cache_mem_k": nrm((DEPTH, DEC_BATCH, N_MEM, MEM_HEADS, MEM_HD)),
        "cache_mem_v": nrm((DEPTH, DEC_BATCH, N_MEM, MEM_HEADS, MEM_HD)),
        "mem_prompt": nrm((BATCH, N_MEM, D_MODEL)),
        "g_mix": gain((DEPTH, D_MODEL)),
        "g_mem": gain((DEPTH, D_MODEL)),
        "g_memkv": gain((DEPTH, D_MODEL)),
        "g_ffn": gain((DEPTH, D_MODEL)),
        "wq_mem": nrm((DEPTH, D_MODEL, MEM_W), D_MODEL ** -0.5),
        "wk_mem": nrm((DEPTH, D_MODEL, MEM_W), D_MODEL ** -0.5),
        "wv_mem": nrm((DEPTH, D_MODEL, MEM_W), D_MODEL ** -0.5),
        "wo_mem": nrm((DEPTH, MEM_W, D_MODEL), MEM_W ** -0.5),
        "qn_mem": gain((DEPTH, MEM_HD)),
        "kn_mem": gain((DEPTH, MEM_HD)),
        "w_ffn_gate": nrm((DEPTH, D_MODEL, D_FF), D_MODEL ** -0.5),
        "w_ffn_up": nrm((DEPTH, D_MODEL, D_FF), D_MODEL ** -0.5),
        "w_ffn_down": nrm((DEPTH, D_FF, D_MODEL), D_FF ** -0.5),
        "w_in_ab": nrm((N_EVEN, D_MODEL, AB_COLS), D_MODEL ** -0.5),
        "mu_shift": jax.random.uniform(next(keys), (N_EVEN, RW_COLS), jnp.float32),
        "rw_w0": jax.random.uniform(next(keys), (N_EVEN, RW_W), jnp.float32, minval=-6.0, maxval=1.0),
        "rw_w2": nrm((N_EVEN, LORA_W, RW_W), 0.5 * LORA_W ** -0.5),
        "rw_a0": nrm((N_EVEN, RW_W), 0.5),
        "rw_a2": nrm((N_EVEN, LORA_A, RW_W), 0.5 * LORA_A ** -0.5),
        "rw_g2": nrm((N_EVEN, LORA_G, RW_W), LORA_G ** -0.5),
        "rw_kk": 0.85 + nrm((N_EVEN, RW_W), 0.05),
        "rw_ka": 1.0 + nrm((N_EVEN, RW_W), 0.05),
        "rw_rk": nrm((N_EVEN, RW_W), 0.1),
        "rw_lnx_w": gain((N_EVEN, RW_W)),
        "rw_lnx_b": nrm((N_EVEN, RW_W), 0.01),
        "lru_conv_w": nrm((N_EVEN, CONV_W, LRU_W), CONV_W ** -0.5),
        "lru_conv_b": nrm((N_EVEN, LRU_W), 0.01),
        "lru_wa": nrm((N_EVEN, LRU_BLOCKS, LRU_BD, LRU_BD), LRU_BD ** -0.5),
        "lru_ba": nrm((N_EVEN, LRU_W), 0.01),
        "lru_wx": nrm((N_EVEN, LRU_BLOCKS, LRU_BD, LRU_BD), LRU_BD ** -0.5),
        "lru_bx": nrm((N_EVEN, LRU_W), 0.01),
        "lru_lambda": jnp.log(u) - jnp.log1p(-u),
        "w_out_ab": nrm((N_EVEN, MIX_W, D_MODEL), MIX_W ** -0.5),
        "w_qkv_sb": nrm((N_ODD, D_MODEL, 3 * SB_W), D_MODEL ** -0.5),
        "w_out_sb": nrm((N_ODD, SB_W, D_MODEL), SB_W ** -0.5),
        "sb_bias": jax.random.uniform(next(keys), (N_ODD, SB_HEADS), jnp.float32,
                                      minval=SB_BIAS_LO, maxval=SB_BIAS_HI),
    }


def reference(x_prompt, x_sample, state_rwkv_wkv, state_rwkv_shift, state_lru_h, state_lru_conv,
              cache_sb_k, cache_sb_v, page_table, cache_mem_k, cache_mem_v, mem_prompt,
              g_mix, g_mem, g_memkv, g_ffn, wq_mem, wk_mem, wv_mem, wo_mem, qn_mem, kn_mem,
              w_ffn_gate, w_ffn_up, w_ffn_down, w_in_ab, mu_shift, rw_w0, rw_w2, rw_a0, rw_a2,
              rw_g2, rw_kk, rw_ka, rw_rk, rw_lnx_w, rw_lnx_b, lru_conv_w, lru_conv_b, lru_wa,
              lru_ba, lru_wx, lru_bx, lru_lambda, w_out_ab, w_qkv_sb, w_out_sb, sb_bias):

    def trunk(x, sample):
        bsz = x.shape[0]
        st = {n: [] for n in ("wkv", "shift", "lru_h", "lru_conv", "sb_k", "sb_v", "mem_k", "mem_v")}
        for i in range(DEPTH):
            h = rms_norm(x, g_mix[i])
            if i % 2 == 0:
                e = i // 2
                if sample:
                    init = (state_rwkv_wkv[e], state_rwkv_shift[e], state_lru_h[e], state_lru_conv[e])
                else:
                    init = (jnp.zeros((bsz, RW_HEADS, RW_HD, RW_HD), x.dtype),
                            jnp.zeros((bsz, RW_COLS), x.dtype),
                            jnp.zeros((bsz, LRU_W), x.dtype),
                            jnp.zeros((bsz, CONV_W - 1, LRU_W), x.dtype))
                mix, (s_wkv, s_shift, s_h, s_conv) = rwkv_lru_mixer(
                    h, init[0], init[1], init[2], init[3], w_in_ab[e], mu_shift[e], rw_w0[e], rw_w2[e],
                    rw_a0[e], rw_a2[e], rw_g2[e], rw_kk[e], rw_ka[e], rw_rk[e], rw_lnx_w[e], rw_lnx_b[e],
                    lru_conv_w[e], lru_conv_b[e], lru_wa[e], lru_ba[e], lru_wx[e], lru_bx[e],
                    lru_lambda[e], w_out_ab[e])
                st["wkv"].append(s_wkv)
                st["shift"].append(s_shift)
                st["lru_h"].append(s_h)
                st["lru_conv"].append(s_conv)
            else:
                o = i // 2
                past = (cache_sb_k, cache_sb_v, o, page_table) if sample else None
                mix, (k_rows, v_rows) = stick_breaking_mixer(h, w_qkv_sb[o], w_out_sb[o], sb_bias[o], past)
                st["sb_k"].append(k_rows)
                st["sb_v"].append(v_rows)
            x = x + mix
            if sample:
                mk, mv = cache_mem_k[i], cache_mem_v[i]
            else:
                mk, mv = memory_kv(mem_prompt, g_memkv[i], wk_mem[i], wv_mem[i], kn_mem[i])
                st["mem_k"].append(mk)
                st["mem_v"].append(mv)
            x = x + memory_xattn(rms_norm(x, g_mem[i]), mk, mv, wq_mem[i], qn_mem[i], wo_mem[i])
            x = x + swiglu(rms_norm(x, g_ffn[i]), w_ffn_gate[i], w_ffn_up[i], w_ffn_down[i])
        return x, st

    y_prompt, sp = trunk(x_prompt, False)
    y_sample, ss = trunk(x_sample, True)
    stack = lambda arrs: jnp.stack(arrs, axis=0)
    return (y_prompt, y_sample,
            stack(sp["wkv"]), stack(ss["wkv"]),
            stack(sp["shift"]), stack(ss["shift"]),
            stack(sp["lru_h"]), stack(ss["lru_h"]),
            stack(sp["lru_conv"]), stack(ss["lru_conv"]),
            stack(sp["sb_k"]), stack(ss["sb_k"]),
            stack(sp["sb_v"]), stack(ss["sb_v"]),
            stack(sp["mem_k"]), stack(sp["mem_v"]))
```

```python
import functools
import math

import jax
import jax.numpy as jnp
from jax import lax
from jax.experimental import pallas as pl
from jax.experimental.pallas import tpu as pltpu

F32 = jnp.float32
BF16 = jnp.bfloat16

NORM_EPS = 1e-6
RW_HD = 64
LANES = 128
SUBLANES = 8
BF16_ROWS = 16
RW_CHUNK = 64
LRU_CHUNK = 256
RW_LN_EPS = RW_HD * 1e-5
LRU_C = 8.0
LRU_BD = 64
CONV_W = 4
SB_HD = 64
PAGE = 128
Q_BLOCK = 128
EXP_M05 = math.exp(-0.5)
VMEM_BYTES_V7X = 64 * 1024 * 1024


def _cparams(sem, vmem_mb):
    assert vmem_mb * 1024 * 1024 < VMEM_BYTES_V7X
    return pltpu.CompilerParams(dimension_semantics=sem, vmem_limit_bytes=vmem_mb * 1024 * 1024)


def _dot(a, b):
    return jnp.dot(a, b, preferred_element_type=F32)


def _dot_nt(a, b):
    return lax.dot_general(a, b, (((1,), (1,)), ((), ())), preferred_element_type=F32)


def _dot_tn(a, b):
    return lax.dot_general(a, b, (((0,), (0,)), ((), ())), preferred_element_type=F32)


def _split2(x):
    hi = x.astype(BF16)
    lo = (x - hi.astype(F32)).astype(BF16)
    return hi, lo


def _split3(x):
    hi = x.astype(BF16)
    r = x - hi.astype(F32)
    mid = r.astype(BF16)
    lo = (r - mid.astype(F32)).astype(BF16)
    return hi, mid, lo


def _dot_xw(x, w_bf16, parts=2):
    pieces = _split2(x) if parts == 2 else _split3(x)
    out = _dot(pieces[0], w_bf16)
    for p in pieces[1:]:
        out = out + _dot(p, w_bf16)
    return out


def _dot_wx(w_bf16, x, parts=3):
    pieces = _split2(x) if parts == 2 else _split3(x)
    out = _dot(w_bf16, pieces[0])
    for p in pieces[1:]:
        out = out + _dot(w_bf16, p)
    return out


def _mm3(a, b, dot):
    ah, al = _split2(a)
    bh, bl = _split2(b)
    return dot(ah, bh) + (dot(al, bh) + dot(ah, bl))


def _mm1(a, b, dot):
    return dot(a.astype(BF16), b.astype(BF16))


def _softplus(x):
    return jnp.maximum(x, 0.0) + jnp.log1p(jnp.exp(-jnp.abs(x)))


def _log_sigmoid(x):
    return jnp.minimum(x, 0.0) - jnp.log1p(jnp.exp(-jnp.abs(x)))


def _gelu_tanh(x):
    return 0.5 * x * (1.0 + jnp.tanh(math.sqrt(2.0 / math.pi) * (x + 0.044715 * (x * x * x))))


def _rms(x, g, eps=NORM_EPS):
    ms = jnp.mean(x * x, axis=-1, keepdims=True)
    return x * lax.rsqrt(ms + eps) * g


def _norm_proj_kernel(x_ref, g_ref, w_ref, *o_refs, splits, chunk):
    h = _rms(x_ref[...], g_ref[...]).astype(BF16)
    for o_ref, (s, e) in zip(o_refs, splits):
        for c0 in range(s, e, chunk):
            c1 = min(c0 + chunk, e)
            o_ref[:, c0 - s:c1 - s] = _dot(h, w_ref[:, c0:c1])


def _norm_proj(x, g, w_bf16, splits, tm):
    m, d = x.shape
    n = w_bf16.shape[1]
    assert m % tm == 0
    bounds = [0] + list(splits) + [n]
    ranges = [(bounds[i], bounds[i + 1]) for i in range(len(bounds) - 1)]
    out_shape = [jax.ShapeDtypeStruct((m, e - s), F32) for s, e in ranges]
    out_specs = [pl.BlockSpec((tm, e - s), lambda i: (i, 0)) for s, e in ranges]
    return pl.pallas_call(
        functools.partial(_norm_proj_kernel, splits=ranges, chunk=512),
        grid=(m // tm,),
        in_specs=[pl.BlockSpec((tm, d), lambda i: (i, 0)),
                  pl.BlockSpec((1, d), lambda i: (0, 0)),
                  pl.BlockSpec((d, n), lambda i: (0, 0))],
        out_specs=out_specs,
        out_shape=out_shape,
        compiler_params=_cparams(("parallel",), 48),
        name="norm_proj",
    )(x, g.reshape(1, d), w_bf16)


def _proj_res_kernel(*refs, n):
    x_ref, a_refs, w_refs, o_ref = refs[0], refs[1:1 + n], refs[1 + n:1 + 2 * n], refs[-1]
    acc = x_ref[...]
    for a_ref, w_ref in zip(a_refs, w_refs):
        acc = acc + _dot(a_ref[...].astype(BF16), w_ref[...])
    o_ref[...] = acc


def _proj_residual(x, a_list, w_list, tm):
    m, d = x.shape
    n = len(a_list)
    assert m % tm == 0
    in_specs = [pl.BlockSpec((tm, d), lambda i: (i, 0))]
    in_specs += [pl.BlockSpec((tm, a.shape[1]), lambda i: (i, 0)) for a in a_list]
    in_specs += [pl.BlockSpec(w.shape, lambda i: (0, 0)) for w in w_list]
    return pl.pallas_call(
        functools.partial(_proj_res_kernel, n=n),
        grid=(m // tm,),
        in_specs=in_specs,
        out_specs=pl.BlockSpec((tm, d), lambda i: (i, 0)),
        out_shape=jax.ShapeDtypeStruct((m, d), F32),
        compiler_params=_cparams(("parallel",), 32),
        name="proj_residual",
    )(x, *a_list, *w_list)


def _ffn_kernel(x_ref, g_ref, wg_ref, wu_ref, wd_ref, o_ref, h_ref):
    x = x_ref[...]
    h_ref[...] = _rms(x, g_ref[...]).astype(BF16)
    o_ref[...] = x

    def body(j, carry):
        h = h_ref[...]
        gate = _dot(h, wg_ref[j])
        up = _dot(h, wu_ref[j])
        act = (gate * jax.nn.sigmoid(gate) * up).astype(BF16)
        o_ref[...] += _dot(act, wd_ref[j])
        return carry

    lax.fori_loop(0, wg_ref.shape[0], body, 0)


def _ffn_residual(x, g, wg3, wu3, wd3, tm):
    m, d = x.shape
    nf, _, tf = wg3.shape
    assert m % tm == 0
    return pl.pallas_call(
        _ffn_kernel,
        grid=(m // tm,),
        in_specs=[pl.BlockSpec((tm, d), lambda i: (i, 0)),
                  pl.BlockSpec((1, d), lambda i: (0, 0)),
                  pl.BlockSpec((nf, d, tf), lambda i: (0, 0, 0)),
                  pl.BlockSpec((nf, d, tf), lambda i: (0, 0, 0)),
                  pl.BlockSpec((nf, tf, d), lambda i: (0, 0, 0))],
        out_specs=pl.BlockSpec((tm, d), lambda i: (i, 0)),
        out_shape=jax.ShapeDtypeStruct((m, d), F32),
        scratch_shapes=[pltpu.VMEM((tm, d), BF16)],
        compiler_params=_cparams(("parallel",), 56),
        name="ffn_residual",
    )(x, g.reshape(1, d), wg3, wu3, wd3)


def _memkv_kernel(m_ref, g_ref, wk_ref, wv_ref, kn_ref, k_ref, v_ref, *, heads, hd):
    mn = _rms(m_ref[0], g_ref[...]).astype(BF16)
    k = _dot(mn, wk_ref[...])
    v_ref[0] = _dot(mn, wv_ref[...])
    for hh in range(heads):
        sl = slice(hh * hd, (hh + 1) * hd)
        k_ref[0, :, sl] = _rms(k[:, sl], kn_ref[...])


def _memory_kv(mem, g, wk, wv, kn, heads):
    b, nm, d = mem.shape
    w = wk.shape[1]
    hd = w // heads
    return pl.pallas_call(
        functools.partial(_memkv_kernel, heads=heads, hd=hd),
        grid=(b,),
        in_specs=[pl.BlockSpec((1, nm, d), lambda i: (i, 0, 0)),
                  pl.BlockSpec((1, d), lambda i: (0, 0)),
                  pl.BlockSpec((d, w), lambda i: (0, 0)),
                  pl.BlockSpec((d, w), lambda i: (0, 0)),
                  pl.BlockSpec((1, hd), lambda i: (0, 0))],
        out_specs=[pl.BlockSpec((1, nm, w), lambda i: (i, 0, 0)),
                   pl.BlockSpec((1, nm, w), lambda i: (i, 0, 0))],
        out_shape=[jax.ShapeDtypeStruct((b, nm, w), F32)] * 2,
        compiler_params=_cparams(("parallel",), 32),
        name="memory_kv",
    )(mem, g.reshape(1, d), wk, wv, kn.reshape(1, hd))


def _memx_kernel(x_ref, g_ref, wq_ref, qn_ref, k_ref, v_ref, wo_ref, o_ref, *, heads, hd):
    x = x_ref[0]
    h = _rms(x, g_ref[...]).astype(BF16)
    q = _dot(h, wq_ref[...])
    acc = x
    scale = hd ** -0.5
    for hh in range(heads):
        sl = slice(hh * hd, (hh + 1) * hd)
        qh = _rms(q[:, sl], qn_ref[...]).astype(BF16)
        s = _dot_nt(qh, k_ref[0, :, sl].astype(BF16)) * scale
        e = jnp.exp(s - jnp.max(s, axis=-1, keepdims=True))
        p = e / jnp.sum(e, axis=-1, keepdims=True)
        oh = _dot(p.astype(BF16), v_ref[0, :, sl].astype(BF16))
        acc = acc + _dot(oh.astype(BF16), wo_ref[sl, :])
    o_ref[0] = acc


def _memx_residual(x, g, wq, qn, k, v, wo, heads, tm):
    b, t, d = x.shape
    nm, w = k.shape[1], k.shape[2]
    hd = w // heads
    assert t % tm == 0
    return pl.pallas_call(
        functools.partial(_memx_kernel, heads=heads, hd=hd),
        grid=(b, t // tm),
        in_specs=[pl.BlockSpec((1, tm, d), lambda i, j: (i, j, 0)),
                  pl.BlockSpec((1, d), lambda i, j: (0, 0)),
                  pl.BlockSpec((d, w), lambda i, j: (0, 0)),
                  pl.BlockSpec((1, hd), lambda i, j: (0, 0)),
                  pl.BlockSpec((1, nm, w), lambda i, j: (i, 0, 0)),
                  pl.BlockSpec((1, nm, w), lambda i, j: (i, 0, 0)),
                  pl.BlockSpec((w, d), lambda i, j: (0, 0))],
        out_specs=pl.BlockSpec((1, tm, d), lambda i, j: (i, j, 0)),
        out_shape=jax.ShapeDtypeStruct((b, t, d), F32),
        compiler_params=_cparams(("parallel", "parallel"), 32),
        name="memx_residual",
    )(x, g.reshape(1, d), wq, qn.reshape(1, hd), k, v, wo)


def _unit_lower_inverse(lmat, n, block):
    row = lax.broadcasted_iota(jnp.int32, (n, n), 0)
    col = lax.broadcasted_iota(jnp.int32, (n, n), 1)
    t = jnp.where(row == col, 1.0, 0.0).astype(F32)
    s = 1
    while s < block:
        lower_left = ((row ^ col) < 2 * s) & ((row & s) != 0) & ((col & s) == 0)
        off = jnp.where(lower_left, lmat, 0.0)
        t = t - _mm3(t, _mm3(off, t, _dot), _dot)
        s *= 2
    return t


def _rwkv_kernel(p_ref, sh0_ref, s0_ref, mu_ref, w0_ref, w2_ref, a0_ref, a2_ref, g2_ref,
                 kk_ref, ka_ref, rk_ref, lw_ref, lb_ref, bd_ref,
                 o_ref, s_out_ref, s_ref, prev_ref, *, c, t_valid, width):
    ci = pl.program_id(1)
    npairs = width // LANES

    @pl.when(ci == 0)
    def _():
        s_ref[...] = s0_ref[0]
        prev_ref[0:1, :] = sh0_ref[0]

    p = p_ref[0]
    row1 = lax.broadcasted_iota(jnp.int32, (c, 1), 0)
    prev = jnp.where(row1 == 0, prev_ref[0:1, :], pltpu.roll(p, 1, axis=0))
    prev_ref[0:1, :] = p[c - 1:c, :]
    xs = p + mu_ref[...] * (prev - p)
    r = xs[:, 0:width]
    k = xs[:, width:2 * width]
    v = xs[:, 2 * width:3 * width]
    dwa = xs[:, 3 * width:3 * width + LANES]
    dg = xs[:, 3 * width + LANES:3 * width + 2 * LANES]

    u = w0_ref[...] + _dot(jnp.tanh(dwa).astype(BF16), w2_ref[...])
    logw = -EXP_M05 * jax.nn.sigmoid(u)
    a = jax.nn.sigmoid(a0_ref[...] + _dot(dwa.astype(BF16), a2_ref[...]))
    g = _dot(jax.nn.sigmoid(dg).astype(BF16), g2_ref[...])

    bd = bd_ref[...]
    kkr = k * kk_ref[...]
    kk = kkr / jnp.maximum(jnp.sqrt(_dot_xw(kkr * kkr, bd)), 1e-12)
    kmod = k * (1.0 + (a - 1.0) * ka_ref[...])
    if t_valid % c != 0:
        valid = ((ci * c + row1) < t_valid).astype(F32)
        logw = logw * valid
        kk = kk * valid
        kmod = kmod * valid
        v = v * valid
    b = kk * a

    rr = lax.broadcasted_iota(jnp.int32, (c, c), 0)
    cc = lax.broadcasted_iota(jnp.int32, (c, c), 1)
    tril = jnp.where(cc <= rr, 1.0, 0.0).astype(BF16)
    lc = _dot_wx(tril, logw, parts=3)
    lc_end = lc[c - 1:c, :]
    c_in = jnp.exp(lc)
    inv_c = jnp.exp(-lc)
    to_end = jnp.exp(lc_end - lc)
    kk_t = kk * jnp.exp(lc - logw)
    b_t = b * inv_c
    k_t = kmod * inv_c
    r_t = r * c_in
    b_end = b * to_end
    k_end = kmod * to_end
    c_end = jnp.exp(lc_end)

    n2 = 2 * c
    lane = lax.broadcasted_iota(jnp.int32, (1, LANES), 1)
    m0 = (lane < RW_HD).astype(F32)
    m1 = 1.0 - m0
    r2 = lax.broadcasted_iota(jnp.int32, (n2, n2), 0)
    c2 = lax.broadcasted_iota(jnp.int32, (n2, n2), 1)
    strict = c2 < r2
    incl = c2 <= r2

    def stack(z):
        return jnp.concatenate([z * m0, z * m1], axis=0)

    ys = []
    for pi in range(npairs):
        sl = slice(pi * LANES, (pi + 1) * LANES)
        kks, bs, ks, rs, vs = stack(kk_t[:, sl]), stack(b_t[:, sl]), stack(k_t[:, sl]), stack(r_t[:, sl]), stack(v[:, sl])
        s_pair = s_ref[pi]
        a_kb = jnp.where(strict, _mm3(kks, bs, _dot_nt), 0.0)
        a_kk = jnp.where(strict, _mm3(kks, ks, _dot_nt), 0.0)
        rhs = -(_mm3(kks, s_pair, _dot_nt) + _mm3(a_kk, vs, _dot))
        sas = _mm3(_unit_lower_inverse(a_kb, n2, c), rhs, _dot)
        a_rb = jnp.where(incl, _mm1(rs, bs, _dot_nt), 0.0)
        a_rk = jnp.where(incl, _mm1(rs, ks, _dot_nt), 0.0)
        y2 = _mm1(rs, s_pair, _dot_nt) + _mm1(a_rb, sas, _dot) + _mm1(a_rk, vs, _dot)
        ys.append(y2[0:c] + y2[c:n2])
        s_ref[pi] = (s_pair * c_end[:, sl] + _mm1(sas, stack(b_end[:, sl]), _dot_tn)
                     + _mm1(vs, stack(k_end[:, sl]), _dot_tn))
    y = jnp.concatenate(ys, axis=1)

    inv_n = 1.0 / RW_HD
    mean = _dot_xw(y, bd) * inv_n
    d = y - mean
    var = _dot_xw(d * d, bd) * inv_n
    yn = d * lax.rsqrt(var + RW_LN_EPS) * lw_ref[...] + lb_ref[...]
    bonus = _dot_xw(r * kmod * rk_ref[...], bd) * v
    o_ref[0] = (yn + bonus) * g

    @pl.when(ci == pl.num_programs(1) - 1)
    def _():
        s_out_ref[0] = s_ref[...]


def _rwkv_mix(p_rw, shift0, s0_pairs, prm, c, t_valid):
    b, tp, cols = p_rw.shape
    width = prm["w0"].shape[1]
    npairs = width // LANES
    assert tp % c == 0 and cols == 3 * width + 2 * LANES
    vec = lambda n: pl.BlockSpec((1, n), lambda i, j: (0, 0))
    mat = lambda r, n: pl.BlockSpec((r, n), lambda i, j: (0, 0))
    return pl.pallas_call(
        functools.partial(_rwkv_kernel, c=c, t_valid=t_valid, width=width),
        grid=(b, tp // c),
        in_specs=[pl.BlockSpec((1, c, cols), lambda i, j: (i, j, 0)),
                  pl.BlockSpec((1, 1, cols), lambda i, j: (i, 0, 0)),
                  pl.BlockSpec((1, npairs, LANES, LANES), lambda i, j: (i, 0, 0, 0)),
                  vec(cols), vec(width), mat(LANES, width), vec(width), mat(LANES, width),
                  mat(LANES, width), vec(width), vec(width), vec(width), vec(width), vec(width),
                  mat(width, width)],
        out_specs=[pl.BlockSpec((1, c, width), lambda i, j: (i, j, 0)),
                   pl.BlockSpec((1, npairs, LANES, LANES), lambda i, j: (i, 0, 0, 0))],
        out_shape=[jax.ShapeDtypeStruct((b, tp, width), F32),
                   jax.ShapeDtypeStruct((b, npairs, LANES, LANES), F32)],
        scratch_shapes=[pltpu.VMEM((npairs, LANES, LANES), F32),
                        pltpu.VMEM((SUBLANES, cols), F32)],
        compiler_params=_cparams(("parallel", "arbitrary"), 32),
        name="rwkv_mix",
    )(p_rw, shift0, s0_pairs, prm["mu"], prm["w0"], prm["w2"], prm["a0"], prm["a2"], prm["g2"],
      prm["kk"], prm["ka"], prm["rk"], prm["lnw"], prm["lnb"], prm["bd"])


def _lru_kernel(px_ref, pg_ref, conv0_ref, h0_ref, cw_ref, cb_ref, wa_ref, ba_ref, wx_ref, bx_ref,
                lam_ref, o_ref, hl_ref, ext_ref, h_ref, *, c, t_valid, width):
    ci = pl.program_id(1)
    npairs = width // LANES

    @pl.when(ci == 0)
    def _():
        ext_ref[0:SUBLANES, :] = conv0_ref[0]
        h_ref[0:1, :] = h0_ref[0]

    px = px_ref[0]
    ext_ref[SUBLANES:SUBLANES + c, :] = px
    acc = None
    for j in range(CONV_W - 1):
        term = cw_ref[j:j + 1, :] * ext_ref[SUBLANES - (CONV_W - 1) + j:SUBLANES - (CONV_W - 1) + j + c, :]
        acc = term if acc is None else acc + term
    xc = cb_ref[...] + (acc + cw_ref[CONV_W - 1:CONV_W, :] * px)
    ext_ref[0:SUBLANES, :] = ext_ref[c:c + SUBLANES, :]

    xcb = xc.astype(BF16)
    ga, gx = [], []
    for pi in range(npairs):
        sl = slice(pi * LANES, (pi + 1) * LANES)
        ga.append(_dot(xcb[:, sl], wa_ref[pi]))
        gx.append(_dot(xcb[:, sl], wx_ref[pi]))
    gate_r = jax.nn.sigmoid(jnp.concatenate(ga, axis=1) + ba_ref[...])
    gate_i = jax.nn.sigmoid(jnp.concatenate(gx, axis=1) + bx_ref[...])
    log_a = (-LRU_C * gate_r) * _softplus(-lam_ref[...])
    a_t = jnp.exp(log_a)
    b_t = jnp.sqrt(-jnp.tanh(log_a) * (a_t * a_t + 1.0)) * (gate_i * xc)

    row = lax.broadcasted_iota(jnp.int32, (c, 1), 0)
    b_t = b_t + jnp.where(row == 0, a_t * h_ref[0:1, :], 0.0)
    d = 1
    while d < c:
        keep = row >= d
        a_sh = jnp.where(keep, pltpu.roll(a_t, d, axis=0), 1.0)
        b_sh = jnp.where(keep, pltpu.roll(b_t, d, axis=0), 0.0)
        b_t = a_t * b_sh + b_t
        a_t = a_t * a_sh
        d *= 2
    hs = b_t
    h_ref[0:1, :] = hs[c - 1:c, :]
    o_ref[0] = hs * _gelu_tanh(pg_ref[0])

    last = t_valid - 1

    @pl.when(ci == last // c)
    def _():
        hl_ref[0] = hs[last % c:last % c + 1, :]


def _lru_mix(p_x, p_gate, conv0_pad, h0, prm, c, t_valid):
    b, tp, width = p_x.shape
    npairs = width // LANES
    assert tp % c == 0 and c >= SUBLANES
    vec = lambda n: pl.BlockSpec((1, n), lambda i, j: (0, 0))
    blk = pl.BlockSpec((npairs, LANES, LANES), lambda i, j: (0, 0, 0))
    return pl.pallas_call(
        functools.partial(_lru_kernel, c=c, t_valid=t_valid, width=width),
        grid=(b, tp // c),
        in_specs=[pl.BlockSpec((1, c, width), lambda i, j: (i, j, 0)),
                  pl.BlockSpec((1, c, width), lambda i, j: (i, j, 0)),
                  pl.BlockSpec((1, SUBLANES, width), lambda i, j: (i, 0, 0)),
                  pl.BlockSpec((1, 1, width), lambda i, j: (i, 0, 0)),
                  pl.BlockSpec((CONV_W, width), lambda i, j: (0, 0)),
                  vec(width), blk, vec(width), blk, vec(width), vec(width)],
        out_specs=[pl.BlockSpec((1, c, width), lambda i, j: (i, j, 0)),
                   pl.BlockSpec((1, 1, width), lambda i, j: (i, 0, 0))],
        out_shape=[jax.ShapeDtypeStruct((b, tp, width), F32),
                   jax.ShapeDtypeStruct((b, 1, width), F32)],
        scratch_shapes=[pltpu.VMEM((c + SUBLANES, width), F32),
                        pltpu.VMEM((SUBLANES, width), F32)],
        compiler_params=_cparams(("parallel", "arbitrary"), 32),
        name="lru_mix",
    )(p_x, p_gate, conv0_pad, h0, prm["cw"], prm["cb"], prm["wa"], prm["ba"], prm["wx"], prm["bx"],
      prm["lam"])


def _sb_scores(qs, kblk, bias2, mask):
    z = _dot_nt(qs, kblk) * (SB_HD ** -0.5) + bias2
    ls = _log_sigmoid(z)
    l1m = ls - z
    if mask is not None:
        l1m = jnp.where(mask, l1m, 0.0)
    return ls, l1m


def _sb_prompt_kernel(bias_ref, q_ref, k_ref, v_ref, o_ref):
    pi = pl.program_id(1)
    qi = pl.program_id(2)
    blk = Q_BLOCK
    lane = lax.broadcasted_iota(jnp.int32, (1, LANES), 1)
    first = lane < SB_HD
    q = q_ref[0]
    qs = jnp.concatenate([jnp.where(first, q, 0.0), jnp.where(first, 0.0, q)], axis=0).astype(BF16)
    rowh = lax.broadcasted_iota(jnp.int32, (2 * blk, 1), 0)
    bias2 = jnp.where(rowh < blk, bias_ref[2 * pi], bias_ref[2 * pi + 1])
    jj = lax.broadcasted_iota(jnp.int32, (blk, blk), 0)
    ss = lax.broadcasted_iota(jnp.int32, (blk, blk), 1)
    upper = jnp.where(jj > ss, 1.0, 0.0).astype(BF16)

    def block(kb, cum, mask):
        start = pl.multiple_of(kb * blk, blk)
        kblk = k_ref[0, pl.ds(start, blk), :].astype(BF16)
        vblk = v_ref[0, pl.ds(start, blk), :].astype(BF16)
        ls, l1m = _sb_scores(qs, kblk, bias2, mask)
        after = cum + _dot_xw(l1m, upper)
        att = jnp.exp(ls + after)
        if mask is not None:
            att = jnp.where(mask, att, 0.0)
        return _dot(att.astype(BF16), vblk), cum + jnp.sum(l1m, axis=-1, keepdims=True)

    qpos = lax.broadcasted_iota(jnp.int32, (2 * blk, blk), 0) & (blk - 1)
    kpos = lax.broadcasted_iota(jnp.int32, (2 * blk, blk), 1)
    acc, cum = block(qi, jnp.zeros((2 * blk, 1), F32), kpos < qpos)

    def body(i, carry):
        acc, cum = carry
        pv, cum = block(qi - 1 - i, cum, None)
        return acc + pv, cum

    acc, cum = lax.fori_loop(0, qi, body, (acc, cum))
    o_ref[0] = jnp.where(first, acc[0:blk], acc[blk:2 * blk])


def _sb_prompt(qkv, bias, heads):
    b, t, w3 = qkv.shape
    w = w3 // 3
    npairs = w // LANES
    assert t % Q_BLOCK == 0 and heads * SB_HD == w
    return pl.pallas_call(
        _sb_prompt_kernel,
        grid=(b, npairs, t // Q_BLOCK),
        in_specs=[pl.BlockSpec(memory_space=pltpu.SMEM),
                  pl.BlockSpec((1, Q_BLOCK, LANES), lambda i, p, j: (i, j, p)),
                  pl.BlockSpec((1, t, LANES), lambda i, p, j: (i, 0, npairs + p)),
                  pl.BlockSpec((1, t, LANES), lambda i, p, j: (i, 0, 2 * npairs + p))],
        out_specs=pl.BlockSpec((1, Q_BLOCK, LANES), lambda i, p, j: (i, j, p)),
        out_shape=jax.ShapeDtypeStruct((b, t, w), F32),
        compiler_params=_cparams(("parallel", "parallel", "arbitrary"), 32),
        name="sb_prompt",
    )(bias, qkv, qkv, qkv)


def _sb_decode_kernel(pt_ref, bias_ref, q_ref, *refs, pages_per_step, heads):
    k_refs = refs[:pages_per_step]
    v_refs = refs[pages_per_step:2 * pages_per_step]
    o_ref, acc_ref, cum_ref = refs[2 * pages_per_step:]
    j = pl.program_id(1)
    w = heads * SB_HD

    @pl.when(j == 0)
    def _():
        acc_ref[...] = jnp.zeros_like(acc_ref)
        cum_ref[...] = jnp.zeros_like(cum_ref)

    hrow = lax.broadcasted_iota(jnp.int32, (heads, w), 0)
    hcol = lax.broadcasted_iota(jnp.int32, (heads, w), 1) >> (SB_HD.bit_length() - 1)
    own = hrow == hcol
    qm = jnp.where(own, q_ref[0], 0.0).astype(BF16)
    jj = lax.broadcasted_iota(jnp.int32, (PAGE, PAGE), 0)
    ss = lax.broadcasted_iota(jnp.int32, (PAGE, PAGE), 1)
    upper = jnp.where(jj > ss, 1.0, 0.0).astype(BF16)
    bias = bias_ref[...]

    acc = acc_ref[...]
    cum = cum_ref[...]
    for i in range(pages_per_step):
        kp = k_refs[i][0].astype(BF16)
        vp = v_refs[i][0].astype(BF16)
        ls, l1m = _sb_scores(qm, kp, bias, None)
        after = cum + _dot_xw(l1m, upper)
        att = jnp.exp(ls + after)
        acc = acc + _dot(att.astype(BF16), vp)
        cum = cum + jnp.sum(l1m, axis=-1, keepdims=True)
    acc_ref[...] = acc
    cum_ref[...] = cum

    @pl.when(j == pl.num_programs(1) - 1)
    def _():
        o_ref[0] = jnp.sum(jnp.where(own, acc, 0.0), axis=0, keepdims=True)


def _sb_decode(q, k_pool, v_pool, page_table, bias, heads, pages_per_step):
    b, _, w = q.shape
    n_pages = page_table.shape[1]
    assert n_pages % pages_per_step == 0 and k_pool.shape[1:] == (PAGE, w)
    steps = n_pages // pages_per_step

    def page_spec(i):
        return pl.BlockSpec((1, PAGE, w),
                            lambda bi, j, pt: (pt[bi, n_pages - 1 - (j * pages_per_step + i)], 0, 0))

    return pl.pallas_call(
        functools.partial(_sb_decode_kernel, pages_per_step=pages_per_step, heads=heads),
        grid_spec=pltpu.PrefetchScalarGridSpec(
            num_scalar_prefetch=1,
            grid=(b, steps),
            in_specs=[pl.BlockSpec((heads, 1), lambda bi, j, pt: (0, 0)),
                      pl.BlockSpec((1, 1, w), lambda bi, j, pt: (bi, 0, 0))]
                     + [page_spec(i) for i in range(pages_per_step)] * 2,
            out_specs=pl.BlockSpec((1, 1, w), lambda bi, j, pt: (bi, 0, 0)),
            scratch_shapes=[pltpu.VMEM((heads, w), F32), pltpu.VMEM((heads, 1), F32)]),
        out_shape=jax.ShapeDtypeStruct((b, 1, w), F32),
        compiler_params=_cparams(("parallel", "arbitrary"), 48),
        name="sb_decode",
    )(page_table, bias.reshape(heads, 1), q, *([k_pool] * pages_per_step), *([v_pool] * pages_per_step))


def _pair_blockdiag(w):
    n, d, _ = w.shape
    w = w.reshape(n // 2, 2, d, d)
    z = jnp.zeros_like(w[:, 0])
    top = jnp.concatenate([w[:, 0], z], axis=2)
    bot = jnp.concatenate([z, w[:, 1]], axis=2)
    return jnp.concatenate([top, bot], axis=1)


def _state_to_pairs(s):
    b, h, n, _ = s.shape
    return _pair_blockdiag(s.reshape(b * h, n, n)).reshape(b, h // 2, 2 * n, 2 * n)


def _pairs_to_state(sp, n):
    b, hp = sp.shape[0], sp.shape[1]
    return jnp.stack([sp[:, :, :n, :n], sp[:, :, n:, n:]], axis=2).reshape(b, 2 * hp, n, n)


def _pad_rows(x, multiple):
    t = x.shape[1]
    t_pad = -(-t // multiple) * multiple
    return x if t_pad == t else jnp.pad(x, ((0, 0), (0, t_pad - t), (0, 0)))


def kernel(x_prompt, x_sample, state_rwkv_wkv, state_rwkv_shift, state_lru_h, state_lru_conv,
           cache_sb_k, cache_sb_v, page_table, cache_mem_k, cache_mem_v, mem_prompt,
           g_mix, g_mem, g_memkv, g_ffn, wq_mem, wk_mem, wv_mem, wo_mem, qn_mem, kn_mem,
           w_ffn_gate, w_ffn_up, w_ffn_down, w_in_ab, mu_shift, rw_w0, rw_w2, rw_a0, rw_a2,
           rw_g2, rw_kk, rw_ka, rw_rk, rw_lnx_w, rw_lnx_b, lru_conv_w, lru_conv_b, lru_wa,
           lru_ba, lru_wx, lru_bx, lru_lambda, w_out_ab, w_qkv_sb, w_out_sb, sb_bias):
    depth, d_model = g_mix.shape
    mem_heads = cache_mem_k.shape[3]
    mem_w = wq_mem.shape[2]
    rw_w = rw_w0.shape[1]
    rw_cols = mu_shift.shape[1]
    lru_w = lru_lambda.shape[1]
    rw_heads = rw_w // RW_HD
    sb_heads = sb_bias.shape[1]
    sb_w = sb_heads * SB_HD
    d_ff = w_ffn_gate.shape[2]
    lora_w = rw_w2.shape[1]
    lora_a = rw_a2.shape[1]
    assert lora_w + lora_a == LANES and rw_g2.shape[1] == LANES
    tf = 256
    assert d_ff % tf == 0
    bf = lambda z: z.astype(BF16)

    wg3 = [bf(w_ffn_gate[i]).reshape(d_model, d_ff // tf, tf).transpose(1, 0, 2) for i in range(depth)]
    wu3 = [bf(w_ffn_up[i]).reshape(d_model, d_ff // tf, tf).transpose(1, 0, 2) for i in range(depth)]
    wd3 = [bf(w_ffn_down[i]).reshape(d_ff // tf, tf, d_model) for i in range(depth)]
    head_id = jnp.arange(rw_w) // RW_HD
    bd = (head_id[:, None] == head_id[None, :]).astype(BF16)

    def even_params(e):
        zw = jnp.zeros((lora_a, rw_w), F32)
        za = jnp.zeros((lora_w, rw_w), F32)
        row = lambda z: z.reshape(1, -1)
        rw = dict(mu=row(mu_shift[e]), w0=row(rw_w0[e]), w2=bf(jnp.concatenate([rw_w2[e], zw], axis=0)),
                  a0=row(rw_a0[e]), a2=bf(jnp.concatenate([za, rw_a2[e]], axis=0)), g2=bf(rw_g2[e]),
                  kk=row(rw_kk[e]), ka=row(rw_ka[e]), rk=row(rw_rk[e]), lnw=row(rw_lnx_w[e]),
                  lnb=row(rw_lnx_b[e]), bd=bd)
        lru = dict(cw=lru_conv_w[e], cb=row(lru_conv_b[e]), wa=bf(_pair_blockdiag(lru_wa[e])),
                   ba=row(lru_ba[e]), wx=bf(_pair_blockdiag(lru_wx[e])), bx=row(lru_bx[e]),
                   lam=row(lru_lambda[e]))
        return rw, lru

    def trunk(x, sample):
        bsz, t, _ = x.shape
        m = bsz * t
        tm = 512 if m % 512 == 0 else m
        out = {}
        xf = x.reshape(m, d_model)
        for i in range(depth):
            if i % 2 == 0:
                e = i // 2
                rw, lru = even_params(e)
                p_rw, p_x, p_gate = _norm_proj(xf, g_mix[i], bf(w_in_ab[e]),
                                               [rw_cols, rw_cols + lru_w], tm)
                p_rw = p_rw.reshape(bsz, t, rw_cols)
                p_x = p_x.reshape(bsz, t, lru_w)
                p_gate = p_gate.reshape(bsz, t, lru_w)
                if sample:
                    wkv0, shift0 = state_rwkv_wkv[e], state_rwkv_shift[e]
                    h0, conv0 = state_lru_h[e], state_lru_conv[e]
                else:
                    wkv0 = jnp.zeros((bsz, rw_heads, RW_HD, RW_HD), F32)
                    shift0 = jnp.zeros((bsz, rw_cols), F32)
                    h0 = jnp.zeros((bsz, lru_w), F32)
                    conv0 = jnp.zeros((bsz, CONV_W - 1, lru_w), F32)
                o_rw, s_pairs = _rwkv_mix(_pad_rows(p_rw, RW_CHUNK), shift0.reshape(bsz, 1, rw_cols),
                                          _state_to_pairs(wkv0), rw, RW_CHUNK, t)
                conv0_pad = jnp.pad(conv0, ((0, 0), (SUBLANES - (CONV_W - 1), 0), (0, 0)))
                px_pad = _pad_rows(p_x, BF16_ROWS)
                c_lru = LRU_CHUNK if px_pad.shape[1] % LRU_CHUNK == 0 else px_pad.shape[1]
                o_lru, h_last = _lru_mix(px_pad, _pad_rows(p_gate, BF16_ROWS), conv0_pad,
                                         h0.reshape(bsz, 1, lru_w), lru, c_lru, t)
                out.setdefault("wkv", []).append(_pairs_to_state(s_pairs, RW_HD))
                out.setdefault("shift", []).append(p_rw[:, t - 1])
                out.setdefault("lru_h", []).append(h_last[:, 0])
                out.setdefault("lru_conv", []).append(
                    jnp.concatenate([conv0, p_x], axis=1)[:, -(CONV_W - 1):])
                w_out = bf(w_out_ab[e])
                xf = _proj_residual(xf, [o_rw[:, :t].reshape(m, rw_w), o_lru[:, :t].reshape(m, lru_w)],
                                    [w_out[:rw_w], w_out[rw_w:]], tm)
            else:
                o = i // 2
                (qkv,) = _norm_proj(xf, g_mix[i], bf(w_qkv_sb[o]), [], tm)
                qkv = qkv.reshape(bsz, t, 3 * sb_w)
                if sample:
                    pool = cache_sb_k.shape[1]
                    k_pool = cache_sb_k.reshape(-1, PAGE, sb_w)
                    v_pool = cache_sb_v.reshape(-1, PAGE, sb_w)
                    att = _sb_decode(qkv[:, :, :sb_w], k_pool, v_pool, page_table + o * pool,
                                     sb_bias[o], sb_heads, 4)
                else:
                    att = _sb_prompt(qkv, sb_bias[o], sb_heads)
                out.setdefault("sb_k", []).append(qkv[:, :, sb_w:2 * sb_w].reshape(bsz, t, sb_heads, SB_HD))
                out.setdefault("sb_v", []).append(qkv[:, :, 2 * sb_w:].reshape(bsz, t, sb_heads, SB_HD))
                xf = _proj_residual(xf, [att.reshape(m, sb_w)], [bf(w_out_sb[o])], tm)
            if sample:
                mk = cache_mem_k[i].reshape(bsz, -1, mem_w)
                mv = cache_mem_v[i].reshape(bsz, -1, mem_w)
            else:
                mk, mv = _memory_kv(mem_prompt, g_memkv[i], bf(wk_mem[i]), bf(wv_mem[i]), kn_mem[i],
                                    mem_heads)
                out.setdefault("mem_k", []).append(mk.reshape(bsz, -1, mem_heads, mem_w // mem_heads))
                out.setdefault("mem_v", []).append(mv.reshape(bsz, -1, mem_heads, mem_w // mem_heads))
            x3 = _pad_rows(xf.reshape(bsz, t, d_model), BF16_ROWS)
            x3 = _memx_residual(x3, g_mem[i], bf(wq_mem[i]), qn_mem[i], mk, mv, bf(wo_mem[i]),
                                mem_heads, 512 if x3.shape[1] % 512 == 0 else x3.shape[1])
            xf = x3[:, :t].reshape(m, d_model)
            xf = _ffn_residual(xf, g_ffn[i], wg3[i], wu3[i], wd3[i], tm)
        return xf.reshape(bsz, t, d_model), out

    y_p, sp = trunk(x_prompt, False)
    y_s, ss = trunk(x_sample, True)
    st = lambda arrs: jnp.stack(arrs, axis=0)
    return (y_p, y_s, st(sp["wkv"]), st(ss["wkv"]), st(sp["shift"]), st(ss["shift"]),
            st(sp["lru_h"]), st(ss["lru_h"]), st(sp["lru_conv"]), st(ss["lru_conv"]),
            st(sp["sb_k"]), st(ss["sb_k"]), st(sp["sb_v"]), st(ss["sb_v"]),
            st(sp["mem_k"]), st(sp["mem_v"]))
```

```python
import functools
import math

import jax
import jax.numpy as jnp
from jax import lax
from jax.experimental import pallas as pl
from jax.experimental.pallas import tpu as pltpu

F32 = jnp.float32
BF16 = jnp.bfloat16

NORM_EPS = 1e-6
RW_HD = 64
LANES = 128
SUBLANES = 8
BF16_ROWS = 16
RW_CHUNK = 64
LRU_CHUNK = 256
RW_LN_EPS = RW_HD * 1e-5
LRU_C = 8.0
LRU_BD = 64
CONV_W = 4
SB_HD = 64
PAGE = 128
SB_TILE = 256
EXP_M05 = math.exp(-0.5)
VMEM_BYTES_V7X = 64 * 1024 * 1024


def _cparams(sem, vmem_mb):
    assert vmem_mb * 1024 * 1024 < VMEM_BYTES_V7X
    return pltpu.CompilerParams(dimension_semantics=sem, vmem_limit_bytes=vmem_mb * 1024 * 1024)


def _dot(a, b):
    return jnp.dot(a, b, preferred_element_type=F32)


def _dot_nt(a, b):
    return lax.dot_general(a, b, (((1,), (1,)), ((), ())), preferred_element_type=F32)


def _dot_tn(a, b):
    return lax.dot_general(a, b, (((0,), (0,)), ((), ())), preferred_element_type=F32)


def _split2(x):
    hi = x.astype(BF16)
    lo = (x - hi.astype(F32)).astype(BF16)
    return hi, lo


def _split3(x):
    hi = x.astype(BF16)
    r = x - hi.astype(F32)
    mid = r.astype(BF16)
    lo = (r - mid.astype(F32)).astype(BF16)
    return hi, mid, lo


def _dot_xw(x, w_bf16, parts=2):
    pieces = _split2(x) if parts == 2 else _split3(x)
    out = _dot(pieces[0], w_bf16)
    for p in pieces[1:]:
        out = out + _dot(p, w_bf16)
    return out


def _dot_wx(w_bf16, x, parts=3):
    pieces = _split2(x) if parts == 2 else _split3(x)
    out = _dot(w_bf16, pieces[0])
    for p in pieces[1:]:
        out = out + _dot(w_bf16, p)
    return out


def _mm3(a, b, dot):
    ah, al = _split2(a)
    bh, bl = _split2(b)
    return dot(ah, bh) + (dot(al, bh) + dot(ah, bl))


def _mm1(a, b, dot):
    return dot(a.astype(BF16), b.astype(BF16))


def _softplus(x):
    return jnp.maximum(x, 0.0) + jnp.log1p(jnp.exp(-jnp.abs(x)))


def _log_sigmoid(x):
    return jnp.minimum(x, 0.0) - jnp.log1p(jnp.exp(-jnp.abs(x)))


def _gelu_tanh(x):
    return 0.5 * x * (1.0 + jnp.tanh(math.sqrt(2.0 / math.pi) * (x + 0.044715 * (x * x * x))))


def _rms(x, g, eps=NORM_EPS):
    ms = jnp.mean(x * x, axis=-1, keepdims=True)
    return x * lax.rsqrt(ms + eps) * g


def _norm_proj_kernel(x_ref, g_ref, w_ref, *o_refs, splits, chunk):
    h = _rms(x_ref[...], g_ref[...]).astype(BF16)
    for o_ref, (s, e) in zip(o_refs, splits):
        for c0 in range(s, e, chunk):
            c1 = min(c0 + chunk, e)
            o_ref[:, c0 - s:c1 - s] = _dot(h, w_ref[:, c0:c1])


def _norm_proj(x, g, w_bf16, splits, tm):
    m, d = x.shape
    n = w_bf16.shape[1]
    assert m % tm == 0
    bounds = [0] + list(splits) + [n]
    ranges = [(bounds[i], bounds[i + 1]) for i in range(len(bounds) - 1)]
    out_shape = [jax.ShapeDtypeStruct((m, e - s), F32) for s, e in ranges]
    out_specs = [pl.BlockSpec((tm, e - s), lambda i: (i, 0)) for s, e in ranges]
    return pl.pallas_call(
        functools.partial(_norm_proj_kernel, splits=ranges, chunk=512),
        grid=(m // tm,),
        in_specs=[pl.BlockSpec((tm, d), lambda i: (i, 0)),
                  pl.BlockSpec((1, d), lambda i: (0, 0)),
                  pl.BlockSpec((d, n), lambda i: (0, 0))],
        out_specs=out_specs,
        out_shape=out_shape,
        compiler_params=_cparams(("parallel",), 48),
        name="norm_proj",
    )(x, g.reshape(1, d), w_bf16)


def _proj_res_kernel(*refs, n):
    x_ref, a_refs, w_refs, o_ref = refs[0], refs[1:1 + n], refs[1 + n:1 + 2 * n], refs[-1]
    acc = x_ref[...]
    for a_ref, w_ref in zip(a_refs, w_refs):
        acc = acc + _dot(a_ref[...].astype(BF16), w_ref[...])
    o_ref[...] = acc


def _proj_residual(x, a_list, w_list, tm):
    m, d = x.shape
    n = len(a_list)
    assert m % tm == 0
    in_specs = [pl.BlockSpec((tm, d), lambda i: (i, 0))]
    in_specs += [pl.BlockSpec((tm, a.shape[1]), lambda i: (i, 0)) for a in a_list]
    in_specs += [pl.BlockSpec(w.shape, lambda i: (0, 0)) for w in w_list]
    return pl.pallas_call(
        functools.partial(_proj_res_kernel, n=n),
        grid=(m // tm,),
        in_specs=in_specs,
        out_specs=pl.BlockSpec((tm, d), lambda i: (i, 0)),
        out_shape=jax.ShapeDtypeStruct((m, d), F32),
        compiler_params=_cparams(("parallel",), 32),
        name="proj_residual",
    )(x, *a_list, *w_list)


def _ffn_kernel(x_ref, g_ref, wg_ref, wu_ref, wd_ref, o_ref, h_ref):
    x = x_ref[...]
    h_ref[...] = _rms(x, g_ref[...]).astype(BF16)
    o_ref[...] = x

    def body(j, carry):
        h = h_ref[...]
        gate = _dot(h, wg_ref[j])
        up = _dot(h, wu_ref[j])
        act = (gate * jax.nn.sigmoid(gate) * up).astype(BF16)
        o_ref[...] += _dot(act, wd_ref[j])
        return carry

    lax.fori_loop(0, wg_ref.shape[0], body, 0)


def _ffn_residual(x, g, wg3, wu3, wd3, tm):
    m, d = x.shape
    nf, _, tf = wg3.shape
    assert m % tm == 0
    return pl.pallas_call(
        _ffn_kernel,
        grid=(m // tm,),
        in_specs=[pl.BlockSpec((tm, d), lambda i: (i, 0)),
                  pl.BlockSpec((1, d), lambda i: (0, 0)),
                  pl.BlockSpec((nf, d, tf), lambda i: (0, 0, 0)),
                  pl.BlockSpec((nf, d, tf), lambda i: (0, 0, 0)),
                  pl.BlockSpec((nf, tf, d), lambda i: (0, 0, 0))],
        out_specs=pl.BlockSpec((tm, d), lambda i: (i, 0)),
        out_shape=jax.ShapeDtypeStruct((m, d), F32),
        scratch_shapes=[pltpu.VMEM((tm, d), BF16)],
        compiler_params=_cparams(("parallel",), 56),
        name="ffn_residual",
    )(x, g.reshape(1, d), wg3, wu3, wd3)


def _memkv_kernel(m_ref, g_ref, wk_ref, wv_ref, kn_ref, k_ref, v_ref, *, heads, hd):
    mn = _rms(m_ref[0], g_ref[...]).astype(BF16)
    k = _dot(mn, wk_ref[...])
    v_ref[0] = _dot(mn, wv_ref[...])
    for hh in range(heads):
        sl = slice(hh * hd, (hh + 1) * hd)
        k_ref[0, :, sl] = _rms(k[:, sl], kn_ref[...])


def _memory_kv(mem, g, wk, wv, kn, heads):
    b, nm, d = mem.shape
    w = wk.shape[1]
    hd = w // heads
    return pl.pallas_call(
        functools.partial(_memkv_kernel, heads=heads, hd=hd),
        grid=(b,),
        in_specs=[pl.BlockSpec((1, nm, d), lambda i: (i, 0, 0)),
                  pl.BlockSpec((1, d), lambda i: (0, 0)),
                  pl.BlockSpec((d, w), lambda i: (0, 0)),
                  pl.BlockSpec((d, w), lambda i: (0, 0)),
                  pl.BlockSpec((1, hd), lambda i: (0, 0))],
        out_specs=[pl.BlockSpec((1, nm, w), lambda i: (i, 0, 0)),
                   pl.BlockSpec((1, nm, w), lambda i: (i, 0, 0))],
        out_shape=[jax.ShapeDtypeStruct((b, nm, w), F32)] * 2,
        compiler_params=_cparams(("parallel",), 32),
        name="memory_kv",
    )(mem, g.reshape(1, d), wk, wv, kn.reshape(1, hd))


def _memx_kernel(x_ref, g_ref, wq_ref, qn_ref, k_ref, v_ref, wo_ref, o_ref, *, heads, hd):
    x = x_ref[0]
    h = _rms(x, g_ref[...]).astype(BF16)
    q = _dot(h, wq_ref[...])
    acc = x
    scale = hd ** -0.5
    for hh in range(heads):
        sl = slice(hh * hd, (hh + 1) * hd)
        qh = _rms(q[:, sl], qn_ref[...]).astype(BF16)
        s = _dot_nt(qh, k_ref[0, :, sl].astype(BF16)) * scale
        e = jnp.exp(s - jnp.max(s, axis=-1, keepdims=True))
        p = e / jnp.sum(e, axis=-1, keepdims=True)
        oh = _dot(p.astype(BF16), v_ref[0, :, sl].astype(BF16))
        acc = acc + _dot(oh.astype(BF16), wo_ref[sl, :])
    o_ref[0] = acc


def _memx_residual(x, g, wq, qn, k, v, wo, heads, tm):
    b, t, d = x.shape
    nm, w = k.shape[1], k.shape[2]
    hd = w // heads
    assert t % tm == 0
    return pl.pallas_call(
        functools.partial(_memx_kernel, heads=heads, hd=hd),
        grid=(b, t // tm),
        in_specs=[pl.BlockSpec((1, tm, d), lambda i, j: (i, j, 0)),
                  pl.BlockSpec((1, d), lambda i, j: (0, 0)),
                  pl.BlockSpec((d, w), lambda i, j: (0, 0)),
                  pl.BlockSpec((1, hd), lambda i, j: (0, 0)),
                  pl.BlockSpec((1, nm, w), lambda i, j: (i, 0, 0)),
                  pl.BlockSpec((1, nm, w), lambda i, j: (i, 0, 0)),
                  pl.BlockSpec((w, d), lambda i, j: (0, 0))],
        out_specs=pl.BlockSpec((1, tm, d), lambda i, j: (i, j, 0)),
        out_shape=jax.ShapeDtypeStruct((b, t, d), F32),
        compiler_params=_cparams(("parallel", "parallel"), 32),
        name="memx_residual",
    )(x, g.reshape(1, d), wq, qn.reshape(1, hd), k, v, wo)


def _unit_lower_inverse(lmats, n, block):
    row = lax.broadcasted_iota(jnp.int32, (n, n), 0)
    col = lax.broadcasted_iota(jnp.int32, (n, n), 1)
    eye = jnp.where(row == col, 1.0, 0.0).astype(F32)
    first = ((row ^ col) < 2) & ((row & 1) != 0) & ((col & 1) == 0)
    ts = [eye - jnp.where(first, l, 0.0) for l in lmats]
    s = 2
    while s < block:
        lower_left = ((row ^ col) < 2 * s) & ((row & s) != 0) & ((col & s) == 0)
        us = [_mm3(jnp.where(lower_left, l, 0.0), t, _dot) for l, t in zip(lmats, ts)]
        ts = [t - _mm3(t, u, _dot) for t, u in zip(ts, us)]
        s *= 2
    return ts


def _rwkv_kernel(p_ref, sh0_ref, s0_ref, mu_ref, w0_ref, w2_ref, a0_ref, a2_ref, g2_ref,
                 kk_ref, ka_ref, rk_ref, lw_ref, lb_ref, bd_ref,
                 o_ref, s_out_ref, s_ref, prev_ref, *, c, t_valid, width):
    ci = pl.program_id(1)
    npairs = width // LANES

    @pl.when(ci == 0)
    def _():
        s_ref[...] = s0_ref[0]
        prev_ref[0:1, :] = sh0_ref[0]

    p = p_ref[0]
    row1 = lax.broadcasted_iota(jnp.int32, (c, 1), 0)
    prev = jnp.where(row1 == 0, prev_ref[0:1, :], pltpu.roll(p, 1, axis=0))
    prev_ref[0:1, :] = p[c - 1:c, :]
    xs = p + mu_ref[...] * (prev - p)
    r = xs[:, 0:width]
    k = xs[:, width:2 * width]
    v = xs[:, 2 * width:3 * width]
    dwa = xs[:, 3 * width:3 * width + LANES]
    dg = xs[:, 3 * width + LANES:3 * width + 2 * LANES]

    u = w0_ref[...] + _dot(jnp.tanh(dwa).astype(BF16), w2_ref[...])
    logw = -EXP_M05 * jax.nn.sigmoid(u)
    a = jax.nn.sigmoid(a0_ref[...] + _dot(dwa.astype(BF16), a2_ref[...]))
    g = _dot(jax.nn.sigmoid(dg).astype(BF16), g2_ref[...])

    bd = bd_ref[...]
    kkr = k * kk_ref[...]
    kk = kkr / jnp.maximum(jnp.sqrt(_dot_xw(kkr * kkr, bd)), 1e-12)
    kmod = k * (1.0 + (a - 1.0) * ka_ref[...])
    if t_valid % c != 0:
        valid = ((ci * c + row1) < t_valid).astype(F32)
        logw = logw * valid
        kk = kk * valid
        kmod = kmod * valid
        v = v * valid
    b = kk * a

    rr = lax.broadcasted_iota(jnp.int32, (c, c), 0)
    cc = lax.broadcasted_iota(jnp.int32, (c, c), 1)
    tril = jnp.where(cc <= rr, 1.0, 0.0).astype(BF16)
    lc = _dot_wx(tril, logw, parts=3)
    lc_end = lc[c - 1:c, :]
    c_in = jnp.exp(lc)
    inv_c = jnp.exp(-lc)
    to_end = jnp.exp(lc_end - lc)
    kk_t = kk * jnp.exp(lc - logw)
    b_t = b * inv_c
    k_t = kmod * inv_c
    r_t = r * c_in
    b_end = b * to_end
    k_end = kmod * to_end
    c_end = jnp.exp(lc_end)

    n2 = 2 * c
    lane = lax.broadcasted_iota(jnp.int32, (1, LANES), 1)
    m0 = (lane < RW_HD).astype(F32)
    m1 = 1.0 - m0
    r2 = lax.broadcasted_iota(jnp.int32, (n2, n2), 0)
    c2 = lax.broadcasted_iota(jnp.int32, (n2, n2), 1)
    strict = c2 < r2
    incl = c2 <= r2

    def stack(z):
        return jnp.concatenate([z * m0, z * m1], axis=0)

    pairs = range(npairs)
    sls = [slice(pi * LANES, (pi + 1) * LANES) for pi in pairs]
    kks = [stack(kk_t[:, sl]) for sl in sls]
    bs = [stack(b_t[:, sl]) for sl in sls]
    ks = [stack(k_t[:, sl]) for sl in sls]
    rs = [stack(r_t[:, sl]) for sl in sls]
    vs = [stack(v[:, sl]) for sl in sls]
    s_old = [s_ref[pi] for pi in pairs]
    a_kb = [jnp.where(strict, _mm3(kks[i], bs[i], _dot_nt), 0.0) for i in pairs]
    a_kk = [jnp.where(strict, _mm3(kks[i], ks[i], _dot_nt), 0.0) for i in pairs]
    rhs = [-(_mm3(kks[i], s_old[i], _dot_nt) + _mm3(a_kk[i], vs[i], _dot)) for i in pairs]
    tinv = _unit_lower_inverse(a_kb, n2, c)
    sas = [_mm3(tinv[i], rhs[i], _dot) for i in pairs]
    a_rb = [jnp.where(incl, _mm1(rs[i], bs[i], _dot_nt), 0.0) for i in pairs]
    a_rk = [jnp.where(incl, _mm1(rs[i], ks[i], _dot_nt), 0.0) for i in pairs]
    y2 = [_mm1(rs[i], s_old[i], _dot_nt) + _mm1(a_rb[i], sas[i], _dot) + _mm1(a_rk[i], vs[i], _dot)
          for i in pairs]
    for i in pairs:
        s_ref[i] = (s_old[i] * c_end[:, sls[i]] + _mm1(sas[i], stack(b_end[:, sls[i]]), _dot_tn)
                    + _mm1(vs[i], stack(k_end[:, sls[i]]), _dot_tn))
    y = jnp.concatenate([z[0:c] + z[c:n2] for z in y2], axis=1)

    inv_n = 1.0 / RW_HD
    mean = _dot_xw(y, bd) * inv_n
    d = y - mean
    var = _dot_xw(d * d, bd) * inv_n
    yn = d * lax.rsqrt(var + RW_LN_EPS) * lw_ref[...] + lb_ref[...]
    bonus = _dot_xw(r * kmod * rk_ref[...], bd) * v
    o_ref[0] = (yn + bonus) * g

    @pl.when(ci == pl.num_programs(1) - 1)
    def _():
        s_out_ref[0] = s_ref[...]


def _rwkv_mix(p_rw, shift0, s0_pairs, prm, c, t_valid):
    b, tp, cols = p_rw.shape
    width = prm["w0"].shape[1]
    npairs = width // LANES
    assert tp % c == 0 and cols == 3 * width + 2 * LANES
    vec = lambda n: pl.BlockSpec((1, n), lambda i, j: (0, 0))
    mat = lambda r, n: pl.BlockSpec((r, n), lambda i, j: (0, 0))
    return pl.pallas_call(
        functools.partial(_rwkv_kernel, c=c, t_valid=t_valid, width=width),
        grid=(b, tp // c),
        in_specs=[pl.BlockSpec((1, c, cols), lambda i, j: (i, j, 0)),
                  pl.BlockSpec((1, 1, cols), lambda i, j: (i, 0, 0)),
                  pl.BlockSpec((1, npairs, LANES, LANES), lambda i, j: (i, 0, 0, 0)),
                  vec(cols), vec(width), mat(LANES, width), vec(width), mat(LANES, width),
                  mat(LANES, width), vec(width), vec(width), vec(width), vec(width), vec(width),
                  mat(width, width)],
        out_specs=[pl.BlockSpec((1, c, width), lambda i, j: (i, j, 0)),
                   pl.BlockSpec((1, npairs, LANES, LANES), lambda i, j: (i, 0, 0, 0))],
        out_shape=[jax.ShapeDtypeStruct((b, tp, width), F32),
                   jax.ShapeDtypeStruct((b, npairs, LANES, LANES), F32)],
        scratch_shapes=[pltpu.VMEM((npairs, LANES, LANES), F32),
                        pltpu.VMEM((SUBLANES, cols), F32)],
        compiler_params=_cparams(("parallel", "arbitrary"), 32),
        name="rwkv_mix",
    )(p_rw, shift0, s0_pairs, prm["mu"], prm["w0"], prm["w2"], prm["a0"], prm["a2"], prm["g2"],
      prm["kk"], prm["ka"], prm["rk"], prm["lnw"], prm["lnb"], prm["bd"])


def _lru_kernel(px_ref, pg_ref, conv0_ref, h0_ref, cw_ref, cb_ref, wa_ref, ba_ref, wx_ref, bx_ref,
                lam_ref, o_ref, hl_ref, ext_ref, h_ref, *, c, t_valid, width):
    ci = pl.program_id(1)
    npairs = width // LANES

    @pl.when(ci == 0)
    def _():
        ext_ref[0:SUBLANES, :] = conv0_ref[0]
        h_ref[0:1, :] = h0_ref[0]

    px = px_ref[0]
    ext_ref[SUBLANES:SUBLANES + c, :] = px
    acc = None
    for j in range(CONV_W - 1):
        term = cw_ref[j:j + 1, :] * ext_ref[SUBLANES - (CONV_W - 1) + j:SUBLANES - (CONV_W - 1) + j + c, :]
        acc = term if acc is None else acc + term
    xc = cb_ref[...] + (acc + cw_ref[CONV_W - 1:CONV_W, :] * px)
    ext_ref[0:SUBLANES, :] = ext_ref[c:c + SUBLANES, :]

    xcb = xc.astype(BF16)
    ga, gx = [], []
    for pi in range(npairs):
        sl = slice(pi * LANES, (pi + 1) * LANES)
        ga.append(_dot(xcb[:, sl], wa_ref[pi]))
        gx.append(_dot(xcb[:, sl], wx_ref[pi]))
    gate_r = jax.nn.sigmoid(jnp.concatenate(ga, axis=1) + ba_ref[...])
    gate_i = jax.nn.sigmoid(jnp.concatenate(gx, axis=1) + bx_ref[...])
    log_a = (-LRU_C * gate_r) * _softplus(-lam_ref[...])
    a_t = jnp.exp(log_a)
    b_t = jnp.sqrt(-jnp.tanh(log_a) * (a_t * a_t + 1.0)) * (gate_i * xc)

    row = lax.broadcasted_iota(jnp.int32, (c, 1), 0)
    b_t = b_t + jnp.where(row == 0, a_t * h_ref[0:1, :], 0.0)
    d = 1
    while d < c:
        keep = row >= d
        a_sh = jnp.where(keep, pltpu.roll(a_t, d, axis=0), 1.0)
        b_sh = jnp.where(keep, pltpu.roll(b_t, d, axis=0), 0.0)
        b_t = a_t * b_sh + b_t
        a_t = a_t * a_sh
        d *= 2
    hs = b_t
    h_ref[0:1, :] = hs[c - 1:c, :]
    o_ref[0] = hs * _gelu_tanh(pg_ref[0])

    last = t_valid - 1

    @pl.when(ci == last // c)
    def _():
        hl_ref[0] = hs[last % c:last % c + 1, :]


def _lru_mix(p_x, p_gate, conv0_pad, h0, prm, c, t_valid):
    b, tp, width = p_x.shape
    npairs = width // LANES
    assert tp % c == 0 and c >= SUBLANES
    vec = lambda n: pl.BlockSpec((1, n), lambda i, j: (0, 0))
    blk = pl.BlockSpec((npairs, LANES, LANES), lambda i, j: (0, 0, 0))
    return pl.pallas_call(
        functools.partial(_lru_kernel, c=c, t_valid=t_valid, width=width),
        grid=(b, tp // c),
        in_specs=[pl.BlockSpec((1, c, width), lambda i, j: (i, j, 0)),
                  pl.BlockSpec((1, c, width), lambda i, j: (i, j, 0)),
                  pl.BlockSpec((1, SUBLANES, width), lambda i, j: (i, 0, 0)),
                  pl.BlockSpec((1, 1, width), lambda i, j: (i, 0, 0)),
                  pl.BlockSpec((CONV_W, width), lambda i, j: (0, 0)),
                  vec(width), blk, vec(width), blk, vec(width), vec(width)],
        out_specs=[pl.BlockSpec((1, c, width), lambda i, j: (i, j, 0)),
                   pl.BlockSpec((1, 1, width), lambda i, j: (i, 0, 0))],
        out_shape=[jax.ShapeDtypeStruct((b, tp, width), F32),
                   jax.ShapeDtypeStruct((b, 1, width), F32)],
        scratch_shapes=[pltpu.VMEM((c + SUBLANES, width), F32),
                        pltpu.VMEM((SUBLANES, width), F32)],
        compiler_params=_cparams(("parallel", "arbitrary"), 32),
        name="lru_mix",
    )(p_x, p_gate, conv0_pad, h0, prm["cw"], prm["cb"], prm["wa"], prm["ba"], prm["wx"], prm["bx"],
      prm["lam"])


def _sb_scores(qs, kblk, bias2, mask):
    z = _dot_nt(qs, kblk) + bias2
    ls = _log_sigmoid(z)
    l1m = ls - z
    if mask is not None:
        l1m = jnp.where(mask, l1m, 0.0)
    return ls, l1m


def _sb_prompt_kernel(bias_ref, q_ref, k_ref, v_ref, o_ref):
    pi = pl.program_id(1)
    qi = pl.program_id(2)
    blk = SB_TILE
    lane = lax.broadcasted_iota(jnp.int32, (1, LANES), 1)
    first = lane < SB_HD
    q = q_ref[0] * (SB_HD ** -0.5)
    qs = jnp.concatenate([jnp.where(first, q, 0.0), jnp.where(first, 0.0, q)], axis=0).astype(BF16)
    rowh = lax.broadcasted_iota(jnp.int32, (2 * blk, 1), 0)
    bias2 = jnp.where(rowh < blk, bias_ref[2 * pi], bias_ref[2 * pi + 1])
    jj = lax.broadcasted_iota(jnp.int32, (blk, blk), 0)
    ss = lax.broadcasted_iota(jnp.int32, (blk, blk), 1)
    upper = jnp.where(jj > ss, 1.0, 0.0).astype(BF16)

    def block(kb, cum, mask):
        start = pl.multiple_of(kb * blk, blk)
        kblk = k_ref[0, pl.ds(start, blk), :].astype(BF16)
        vblk = v_ref[0, pl.ds(start, blk), :].astype(BF16)
        ls, l1m = _sb_scores(qs, kblk, bias2, mask)
        after = cum + _dot_xw(l1m, upper)
        att = jnp.exp(ls + after)
        if mask is not None:
            att = jnp.where(mask, att, 0.0)
        return _dot(att.astype(BF16), vblk), cum + jnp.sum(l1m, axis=-1, keepdims=True)

    qpos = lax.broadcasted_iota(jnp.int32, (2 * blk, blk), 0) & (blk - 1)
    kpos = lax.broadcasted_iota(jnp.int32, (2 * blk, blk), 1)
    acc, cum = block(qi, jnp.zeros((2 * blk, 1), F32), kpos < qpos)

    def body(i, carry):
        acc, cum = carry
        pv, cum = block(qi - 1 - i, cum, None)
        return acc + pv, cum

    acc, cum = lax.fori_loop(0, qi, body, (acc, cum))
    o_ref[0] = jnp.where(first, acc[0:blk], acc[blk:2 * blk])


def _sb_prompt(qkv, bias, heads):
    b, t, w3 = qkv.shape
    w = w3 // 3
    npairs = w // LANES
    assert t % SB_TILE == 0 and heads * SB_HD == w
    return pl.pallas_call(
        _sb_prompt_kernel,
        grid=(b, npairs, t // SB_TILE),
        in_specs=[pl.BlockSpec(memory_space=pltpu.SMEM),
                  pl.BlockSpec((1, SB_TILE, LANES), lambda i, p, j: (i, j, p)),
                  pl.BlockSpec((1, t, LANES), lambda i, p, j: (i, 0, npairs + p)),
                  pl.BlockSpec((1, t, LANES), lambda i, p, j: (i, 0, 2 * npairs + p))],
        out_specs=pl.BlockSpec((1, SB_TILE, LANES), lambda i, p, j: (i, j, p)),
        out_shape=jax.ShapeDtypeStruct((b, t, w), F32),
        compiler_params=_cparams(("parallel", "parallel", "arbitrary"), 32),
        name="sb_prompt",
    )(bias, qkv, qkv, qkv)


def _sb_decode_kernel(pt_ref, bias_ref, q_ref, *refs, pages_per_step, heads):
    k_refs = refs[:pages_per_step]
    v_refs = refs[pages_per_step:2 * pages_per_step]
    o_ref, acc_ref, cum_ref = refs[2 * pages_per_step:]
    j = pl.program_id(1)
    rows = PAGE * heads
    tiles = rows // LANES
    hbits = heads.bit_length() - 1

    @pl.when(j == 0)
    def _():
        acc_ref[...] = jnp.zeros_like(acc_ref)
        cum_ref[...] = jnp.zeros_like(cum_ref)

    qb = (q_ref[0] * (SB_HD ** -0.5)).astype(BF16)
    hrow = lax.broadcasted_iota(jnp.int32, (heads, LANES), 0)
    hlane = lax.broadcasted_iota(jnp.int32, (heads, LANES), 1) & (heads - 1)
    own = hrow == hlane
    l0 = lax.broadcasted_iota(jnp.int32, (LANES, LANES), 0)
    l1 = lax.broadcasted_iota(jnp.int32, (LANES, LANES), 1)
    same_head = (l0 & (heads - 1)) == (l1 & (heads - 1))
    later_in_tile = jnp.where(same_head & ((l0 >> hbits) > (l1 >> hbits)), 1.0, 0.0).astype(BF16)
    tile_total = jnp.where(same_head, 1.0, 0.0).astype(BF16)
    t0 = lax.broadcasted_iota(jnp.int32, (tiles, tiles), 0)
    t1 = lax.broadcasted_iota(jnp.int32, (tiles, tiles), 1)
    later_tiles = jnp.where(t1 > t0, 1.0, 0.0).astype(BF16)
    bias = bias_ref[...]

    pages = range(pages_per_step)
    zalls = [_dot_nt(qb, k_refs[i][...].astype(BF16)) for i in pages]
    zs = [jnp.concatenate(
        [jnp.sum(jnp.where(own, za[:, t * LANES:(t + 1) * LANES], 0.0), axis=0, keepdims=True)
         for t in range(tiles)], axis=0) + bias for za in zalls]
    lss = [_log_sigmoid(z) for z in zs]
    l1ms = [ls - z for ls, z in zip(lss, zs)]
    tots = [_dot_xw(l1m, tile_total) for l1m in l1ms]
    inner = [_dot_xw(l1m, later_in_tile) + _dot_wx(later_tiles, tot, parts=2)
             for l1m, tot in zip(l1ms, tots)]
    acc = acc_ref[...]
    cum = cum_ref[...]
    for i in pages:
        att = jnp.exp(lss[i] + (cum + inner[i]))
        attm = jnp.concatenate([jnp.where(own, att[t:t + 1, :], 0.0) for t in range(tiles)],
                               axis=1).astype(BF16)
        acc = acc + _dot(attm, v_refs[i][...].astype(BF16))
        cum = cum + jnp.sum(tots[i], axis=0, keepdims=True)
    acc_ref[...] = acc
    cum_ref[...] = cum

    @pl.when(j == pl.num_programs(1) - 1)
    def _():
        o_ref[0] = acc


def _sb_decode(q, k_rows, v_rows, page_table, bias, heads, pages_per_step):
    b = q.shape[0]
    n_pages = page_table.shape[1]
    rows = PAGE * heads
    assert n_pages % pages_per_step == 0 and LANES % heads == 0 and heads & (heads - 1) == 0
    steps = n_pages // pages_per_step

    def page_spec(i):
        return pl.BlockSpec((rows, SB_HD),
                            lambda bi, j, pt: (pt[bi, n_pages - 1 - (j * pages_per_step + i)], 0))

    return pl.pallas_call(
        functools.partial(_sb_decode_kernel, pages_per_step=pages_per_step, heads=heads),
        grid_spec=pltpu.PrefetchScalarGridSpec(
            num_scalar_prefetch=1,
            grid=(b, steps),
            in_specs=[pl.BlockSpec((1, LANES), lambda bi, j, pt: (0, 0)),
                      pl.BlockSpec((1, heads, SB_HD), lambda bi, j, pt: (bi, 0, 0))]
                     + [page_spec(i) for i in range(pages_per_step)] * 2,
            out_specs=pl.BlockSpec((1, heads, SB_HD), lambda bi, j, pt: (bi, 0, 0)),
            scratch_shapes=[pltpu.VMEM((heads, SB_HD), F32), pltpu.VMEM((1, LANES), F32)]),
        out_shape=jax.ShapeDtypeStruct((b, heads, SB_HD), F32),
        compiler_params=_cparams(("parallel", "arbitrary"), 48),
        name="sb_decode",
    )(page_table, jnp.tile(bias, LANES // heads).reshape(1, LANES), q,
      *([k_rows] * pages_per_step), *([v_rows] * pages_per_step))


def _pair_blockdiag(w):
    n, d, _ = w.shape
    w = w.reshape(n // 2, 2, d, d)
    z = jnp.zeros_like(w[:, 0])
    top = jnp.concatenate([w[:, 0], z], axis=2)
    bot = jnp.concatenate([z, w[:, 1]], axis=2)
    return jnp.concatenate([top, bot], axis=1)


def _state_to_pairs(s):
    b, h, n, _ = s.shape
    return _pair_blockdiag(s.reshape(b * h, n, n)).reshape(b, h // 2, 2 * n, 2 * n)


def _pairs_to_state(sp, n):
    b, hp = sp.shape[0], sp.shape[1]
    return jnp.stack([sp[:, :, :n, :n], sp[:, :, n:, n:]], axis=2).reshape(b, 2 * hp, n, n)


def _pad_rows(x, multiple):
    t = x.shape[1]
    t_pad = -(-t // multiple) * multiple
    return x if t_pad == t else jnp.pad(x, ((0, 0), (0, t_pad - t), (0, 0)))


def kernel(x_prompt, x_sample, state_rwkv_wkv, state_rwkv_shift, state_lru_h, state_lru_conv,
           cache_sb_k, cache_sb_v, page_table, cache_mem_k, cache_mem_v, mem_prompt,
           g_mix, g_mem, g_memkv, g_ffn, wq_mem, wk_mem, wv_mem, wo_mem, qn_mem, kn_mem,
           w_ffn_gate, w_ffn_up, w_ffn_down, w_in_ab, mu_shift, rw_w0, rw_w2, rw_a0, rw_a2,
           rw_g2, rw_kk, rw_ka, rw_rk, rw_lnx_w, rw_lnx_b, lru_conv_w, lru_conv_b, lru_wa,
           lru_ba, lru_wx, lru_bx, lru_lambda, w_out_ab, w_qkv_sb, w_out_sb, sb_bias):
    depth, d_model = g_mix.shape
    mem_heads = cache_mem_k.shape[3]
    mem_w = wq_mem.shape[2]
    rw_w = rw_w0.shape[1]
    rw_cols = mu_shift.shape[1]
    lru_w = lru_lambda.shape[1]
    rw_heads = rw_w // RW_HD
    sb_heads = sb_bias.shape[1]
    sb_w = sb_heads * SB_HD
    d_ff = w_ffn_gate.shape[2]
    lora_w = rw_w2.shape[1]
    lora_a = rw_a2.shape[1]
    assert lora_w + lora_a == LANES and rw_g2.shape[1] == LANES
    tf = 256
    assert d_ff % tf == 0
    bf = lambda z: z.astype(BF16)

    wg3 = [bf(w_ffn_gate[i]).reshape(d_model, d_ff // tf, tf).transpose(1, 0, 2) for i in range(depth)]
    wu3 = [bf(w_ffn_up[i]).reshape(d_model, d_ff // tf, tf).transpose(1, 0, 2) for i in range(depth)]
    wd3 = [bf(w_ffn_down[i]).reshape(d_ff // tf, tf, d_model) for i in range(depth)]
    head_id = jnp.arange(rw_w) // RW_HD
    bd = (head_id[:, None] == head_id[None, :]).astype(BF16)

    def even_params(e):
        zw = jnp.zeros((lora_a, rw_w), F32)
        za = jnp.zeros((lora_w, rw_w), F32)
        row = lambda z: z.reshape(1, -1)
        rw = dict(mu=row(mu_shift[e]), w0=row(rw_w0[e]), w2=bf(jnp.concatenate([rw_w2[e], zw], axis=0)),
                  a0=row(rw_a0[e]), a2=bf(jnp.concatenate([za, rw_a2[e]], axis=0)), g2=bf(rw_g2[e]),
                  kk=row(rw_kk[e]), ka=row(rw_ka[e]), rk=row(rw_rk[e]), lnw=row(rw_lnx_w[e]),
                  lnb=row(rw_lnx_b[e]), bd=bd)
        lru = dict(cw=lru_conv_w[e], cb=row(lru_conv_b[e]), wa=bf(_pair_blockdiag(lru_wa[e])),
                   ba=row(lru_ba[e]), wx=bf(_pair_blockdiag(lru_wx[e])), bx=row(lru_bx[e]),
                   lam=row(lru_lambda[e]))
        return rw, lru

    def trunk(x, sample):
        bsz, t, _ = x.shape
        m = bsz * t
        tm = 512 if m % 512 == 0 else m
        out = {}
        xf = x.reshape(m, d_model)
        for i in range(depth):
            if i % 2 == 0:
                e = i // 2
                rw, lru = even_params(e)
                p_rw, p_x, p_gate = _norm_proj(xf, g_mix[i], bf(w_in_ab[e]),
                                               [rw_cols, rw_cols + lru_w], tm)
                p_rw = p_rw.reshape(bsz, t, rw_cols)
                p_x = p_x.reshape(bsz, t, lru_w)
                p_gate = p_gate.reshape(bsz, t, lru_w)
                if sample:
                    wkv0, shift0 = state_rwkv_wkv[e], state_rwkv_shift[e]
                    h0, conv0 = state_lru_h[e], state_lru_conv[e]
                else:
                    wkv0 = jnp.zeros((bsz, rw_heads, RW_HD, RW_HD), F32)
                    shift0 = jnp.zeros((bsz, rw_cols), F32)
                    h0 = jnp.zeros((bsz, lru_w), F32)
                    conv0 = jnp.zeros((bsz, CONV_W - 1, lru_w), F32)
                o_rw, s_pairs = _rwkv_mix(_pad_rows(p_rw, RW_CHUNK), shift0.reshape(bsz, 1, rw_cols),
                                          _state_to_pairs(wkv0), rw, RW_CHUNK, t)
                conv0_pad = jnp.pad(conv0, ((0, 0), (SUBLANES - (CONV_W - 1), 0), (0, 0)))
                px_pad = _pad_rows(p_x, BF16_ROWS)
                c_lru = LRU_CHUNK if px_pad.shape[1] % LRU_CHUNK == 0 else px_pad.shape[1]
                o_lru, h_last = _lru_mix(px_pad, _pad_rows(p_gate, BF16_ROWS), conv0_pad,
                                         h0.reshape(bsz, 1, lru_w), lru, c_lru, t)
                out.setdefault("wkv", []).append(_pairs_to_state(s_pairs, RW_HD))
                out.setdefault("shift", []).append(p_rw[:, t - 1])
                out.setdefault("lru_h", []).append(h_last[:, 0])
                out.setdefault("lru_conv", []).append(
                    jnp.concatenate([conv0, p_x], axis=1)[:, -(CONV_W - 1):])
                w_out = bf(w_out_ab[e])
                xf = _proj_residual(xf, [o_rw[:, :t].reshape(m, rw_w), o_lru[:, :t].reshape(m, lru_w)],
                                    [w_out[:rw_w], w_out[rw_w:]], tm)
            else:
                o = i // 2
                (qkv,) = _norm_proj(xf, g_mix[i], bf(w_qkv_sb[o]), [], tm)
                qkv = qkv.reshape(bsz, t, 3 * sb_w)
                if sample:
                    pool = cache_sb_k.shape[1]
                    att = _sb_decode(qkv[:, 0, :sb_w].reshape(bsz, sb_heads, SB_HD),
                                     cache_sb_k.reshape(-1, SB_HD), cache_sb_v.reshape(-1, SB_HD),
                                     page_table + o * pool, sb_bias[o], sb_heads, 4)
                else:
                    att = _sb_prompt(qkv, sb_bias[o], sb_heads)
                out.setdefault("sb_k", []).append(qkv[:, :, sb_w:2 * sb_w].reshape(bsz, t, sb_heads, SB_HD))
                out.setdefault("sb_v", []).append(qkv[:, :, 2 * sb_w:].reshape(bsz, t, sb_heads, SB_HD))
                xf = _proj_residual(xf, [att.reshape(m, sb_w)], [bf(w_out_sb[o])], tm)
            if sample:
                mk = cache_mem_k[i].reshape(bsz, -1, mem_w)
                mv = cache_mem_v[i].reshape(bsz, -1, mem_w)
            else:
                mk, mv = _memory_kv(mem_prompt, g_memkv[i], bf(wk_mem[i]), bf(wv_mem[i]), kn_mem[i],
                                    mem_heads)
                out.setdefault("mem_k", []).append(mk.reshape(bsz, -1, mem_heads, mem_w // mem_heads))
                out.setdefault("mem_v", []).append(mv.reshape(bsz, -1, mem_heads, mem_w // mem_heads))
            x3 = _pad_rows(xf.reshape(bsz, t, d_model), BF16_ROWS)
            x3 = _memx_residual(x3, g_mem[i], bf(wq_mem[i]), qn_mem[i], mk, mv, bf(wo_mem[i]),
                                mem_heads, 512 if x3.shape[1] % 512 == 0 else x3.shape[1])
            xf = x3[:, :t].reshape(m, d_model)
            xf = _ffn_residual(xf, g_ffn[i], wg3[i], wu3[i], wd3[i], tm)
        return xf.reshape(bsz, t, d_model), out

    y_p, sp = trunk(x_prompt, False)
    y_s, ss = trunk(x_sample, True)
    st = lambda arrs: jnp.stack(arrs, axis=0)
    return (y_p, y_s, st(sp["wkv"]), st(ss["wkv"]), st(sp["shift"]), st(ss["shift"]),
            st(sp["lru_h"]), st(ss["lru_h"]), st(sp["lru_conv"]), st(ss["lru_conv"]),
            st(sp["sb_k"]), st(ss["sb_k"]), st(sp["sb_v"]), st(ss["sb_v"]),
            st(sp["mem_k"]), st(sp["mem_v"]))
```

```python
import functools
import math

import jax
import jax.numpy as jnp
from jax import lax
from jax.experimental import pallas as pl
from jax.experimental.pallas import tpu as pltpu

F32 = jnp.float32
BF16 = jnp.bfloat16

NORM_EPS = 1e-6
RW_HD = 64
LANES = 128
SUBLANES = 8
BF16_ROWS = 16
RW_CHUNK = 64
LRU_CHUNK = 256
RW_LN_EPS = RW_HD * 1e-5
LRU_C = 8.0
LRU_BD = 64
CONV_W = 4
SB_HD = 64
PAGE = 128
SB_TILE = 256
EXP_M05 = math.exp(-0.5)
VMEM_BYTES_V7X = 64 * 1024 * 1024


def _cparams(sem, vmem_mb):
    assert vmem_mb * 1024 * 1024 < VMEM_BYTES_V7X
    return pltpu.CompilerParams(dimension_semantics=sem, vmem_limit_bytes=vmem_mb * 1024 * 1024)


def _dot(a, b):
    return jnp.dot(a, b, preferred_element_type=F32)


def _dot_nt(a, b):
    return lax.dot_general(a, b, (((1,), (1,)), ((), ())), preferred_element_type=F32)


def _dot_tn(a, b):
    return lax.dot_general(a, b, (((0,), (0,)), ((), ())), preferred_element_type=F32)


def _split2(x):
    hi = x.astype(BF16)
    lo = (x - hi.astype(F32)).astype(BF16)
    return hi, lo


def _split3(x):
    hi = x.astype(BF16)
    r = x - hi.astype(F32)
    mid = r.astype(BF16)
    lo = (r - mid.astype(F32)).astype(BF16)
    return hi, mid, lo


def _dot_xw(x, w_bf16, parts=2):
    pieces = _split2(x) if parts == 2 else _split3(x)
    out = _dot(pieces[0], w_bf16)
    for p in pieces[1:]:
        out = out + _dot(p, w_bf16)
    return out


def _dot_wx(w_bf16, x, parts=3):
    pieces = _split2(x) if parts == 2 else _split3(x)
    out = _dot(w_bf16, pieces[0])
    for p in pieces[1:]:
        out = out + _dot(w_bf16, p)
    return out


def _mm3(a, b, dot):
    ah, al = _split2(a)
    bh, bl = _split2(b)
    return dot(ah, bh) + (dot(al, bh) + dot(ah, bl))


def _mm1(a, b, dot):
    return dot(a.astype(BF16), b.astype(BF16))


def _softplus(x):
    return jnp.maximum(x, 0.0) + jnp.log1p(jnp.exp(-jnp.abs(x)))


def _gelu_tanh(x):
    return 0.5 * x * (1.0 + jnp.tanh(math.sqrt(2.0 / math.pi) * (x + 0.044715 * (x * x * x))))


def _rms(x, g, eps=NORM_EPS):
    ms = jnp.mean(x * x, axis=-1, keepdims=True)
    return x * lax.rsqrt(ms + eps) * g


def _norm_proj_kernel(x_ref, g_ref, w_ref, *o_refs, splits, chunk):
    h = _rms(x_ref[...], g_ref[...]).astype(BF16)
    for o_ref, (s, e) in zip(o_refs, splits):
        for c0 in range(s, e, chunk):
            c1 = min(c0 + chunk, e)
            o_ref[:, c0 - s:c1 - s] = _dot(h, w_ref[:, c0:c1])


def _norm_proj(x, g, w_bf16, splits, tm):
    m, d = x.shape
    n = w_bf16.shape[1]
    assert m % tm == 0
    bounds = [0] + list(splits) + [n]
    ranges = [(bounds[i], bounds[i + 1]) for i in range(len(bounds) - 1)]
    out_shape = [jax.ShapeDtypeStruct((m, e - s), F32) for s, e in ranges]
    out_specs = [pl.BlockSpec((tm, e - s), lambda i: (i, 0)) for s, e in ranges]
    return pl.pallas_call(
        functools.partial(_norm_proj_kernel, splits=ranges, chunk=512),
        grid=(m // tm,),
        in_specs=[pl.BlockSpec((tm, d), lambda i: (i, 0)),
                  pl.BlockSpec((1, d), lambda i: (0, 0)),
                  pl.BlockSpec((d, n), lambda i: (0, 0))],
        out_specs=out_specs,
        out_shape=out_shape,
        compiler_params=_cparams(("parallel",), 48),
        name="norm_proj",
    )(x, g.reshape(1, d), w_bf16)


def _proj_res_kernel(*refs, n):
    x_ref, a_refs, w_refs, o_ref = refs[0], refs[1:1 + n], refs[1 + n:1 + 2 * n], refs[-1]
    acc = x_ref[...]
    for a_ref, w_ref in zip(a_refs, w_refs):
        acc = acc + _dot(a_ref[...].astype(BF16), w_ref[...])
    o_ref[...] = acc


def _proj_residual(x, a_list, w_list, tm):
    m, d = x.shape
    n = len(a_list)
    assert m % tm == 0
    in_specs = [pl.BlockSpec((tm, d), lambda i: (i, 0))]
    in_specs += [pl.BlockSpec((tm, a.shape[1]), lambda i: (i, 0)) for a in a_list]
    in_specs += [pl.BlockSpec(w.shape, lambda i: (0, 0)) for w in w_list]
    return pl.pallas_call(
        functools.partial(_proj_res_kernel, n=n),
        grid=(m // tm,),
        in_specs=in_specs,
        out_specs=pl.BlockSpec((tm, d), lambda i: (i, 0)),
        out_shape=jax.ShapeDtypeStruct((m, d), F32),
        compiler_params=_cparams(("parallel",), 32),
        name="proj_residual",
    )(x, *a_list, *w_list)


def _ffn_kernel(x_ref, g_ref, wg_ref, wu_ref, wd_ref, o_ref, h_ref):
    x = x_ref[...]
    h_ref[...] = _rms(x, g_ref[...]).astype(BF16)
    o_ref[...] = x

    def body(j, carry):
        h = h_ref[...]
        gate = _dot(h, wg_ref[j])
        up = _dot(h, wu_ref[j])
        act = (gate * jax.nn.sigmoid(gate) * up).astype(BF16)
        o_ref[...] += _dot(act, wd_ref[j])
        return carry

    lax.fori_loop(0, wg_ref.shape[0], body, 0)


def _ffn_residual(x, g, wg3, wu3, wd3, tm):
    m, d = x.shape
    nf, _, tf = wg3.shape
    assert m % tm == 0
    return pl.pallas_call(
        _ffn_kernel,
        grid=(m // tm,),
        in_specs=[pl.BlockSpec((tm, d), lambda i: (i, 0)),
                  pl.BlockSpec((1, d), lambda i: (0, 0)),
                  pl.BlockSpec((nf, d, tf), lambda i: (0, 0, 0)),
                  pl.BlockSpec((nf, d, tf), lambda i: (0, 0, 0)),
                  pl.BlockSpec((nf, tf, d), lambda i: (0, 0, 0))],
        out_specs=pl.BlockSpec((tm, d), lambda i: (i, 0)),
        out_shape=jax.ShapeDtypeStruct((m, d), F32),
        scratch_shapes=[pltpu.VMEM((tm, d), BF16)],
        compiler_params=_cparams(("parallel",), 56),
        name="ffn_residual",
    )(x, g.reshape(1, d), wg3, wu3, wd3)


def _memkv_kernel(m_ref, g_ref, wk_ref, wv_ref, kn_ref, k_ref, v_ref, *, heads, hd):
    mn = _rms(m_ref[0], g_ref[...]).astype(BF16)
    k = _dot(mn, wk_ref[...])
    v_ref[0] = _dot(mn, wv_ref[...])
    for hh in range(heads):
        sl = slice(hh * hd, (hh + 1) * hd)
        k_ref[0, :, sl] = _rms(k[:, sl], kn_ref[...])


def _memory_kv(mem, g, wk, wv, kn, heads):
    b, nm, d = mem.shape
    w = wk.shape[1]
    hd = w // heads
    return pl.pallas_call(
        functools.partial(_memkv_kernel, heads=heads, hd=hd),
        grid=(b,),
        in_specs=[pl.BlockSpec((1, nm, d), lambda i: (i, 0, 0)),
                  pl.BlockSpec((1, d), lambda i: (0, 0)),
                  pl.BlockSpec((d, w), lambda i: (0, 0)),
                  pl.BlockSpec((d, w), lambda i: (0, 0)),
                  pl.BlockSpec((1, hd), lambda i: (0, 0))],
        out_specs=[pl.BlockSpec((1, nm, w), lambda i: (i, 0, 0)),
                   pl.BlockSpec((1, nm, w), lambda i: (i, 0, 0))],
        out_shape=[jax.ShapeDtypeStruct((b, nm, w), F32)] * 2,
        compiler_params=_cparams(("parallel",), 32),
        name="memory_kv",
    )(mem, g.reshape(1, d), wk, wv, kn.reshape(1, hd))


def _memx_kernel(x_ref, g_ref, wq_ref, qn_ref, k_ref, v_ref, wo_ref, o_ref, *, heads, hd):
    x = x_ref[0]
    h = _rms(x, g_ref[...]).astype(BF16)
    q = _dot(h, wq_ref[...])
    acc = x
    scale = hd ** -0.5
    for hh in range(heads):
        sl = slice(hh * hd, (hh + 1) * hd)
        qh = _rms(q[:, sl], qn_ref[...]).astype(BF16)
        s = _dot_nt(qh, k_ref[0, :, sl].astype(BF16)) * scale
        e = jnp.exp(s - jnp.max(s, axis=-1, keepdims=True))
        p = e / jnp.sum(e, axis=-1, keepdims=True)
        oh = _dot(p.astype(BF16), v_ref[0, :, sl].astype(BF16))
        acc = acc + _dot(oh.astype(BF16), wo_ref[sl, :])
    o_ref[0] = acc


def _memx_residual(x, g, wq, qn, k, v, wo, heads, tm):
    b, t, d = x.shape
    nm, w = k.shape[1], k.shape[2]
    hd = w // heads
    assert t % tm == 0
    return pl.pallas_call(
        functools.partial(_memx_kernel, heads=heads, hd=hd),
        grid=(b, t // tm),
        in_specs=[pl.BlockSpec((1, tm, d), lambda i, j: (i, j, 0)),
                  pl.BlockSpec((1, d), lambda i, j: (0, 0)),
                  pl.BlockSpec((d, w), lambda i, j: (0, 0)),
                  pl.BlockSpec((1, hd), lambda i, j: (0, 0)),
                  pl.BlockSpec((1, nm, w), lambda i, j: (i, 0, 0)),
                  pl.BlockSpec((1, nm, w), lambda i, j: (i, 0, 0)),
                  pl.BlockSpec((w, d), lambda i, j: (0, 0))],
        out_specs=pl.BlockSpec((1, tm, d), lambda i, j: (i, j, 0)),
        out_shape=jax.ShapeDtypeStruct((b, t, d), F32),
        compiler_params=_cparams(("parallel", "parallel"), 32),
        name="memx_residual",
    )(x, g.reshape(1, d), wq, qn.reshape(1, hd), k, v, wo)


def _unit_lower_inverse(lmats, n, block):
    row = lax.broadcasted_iota(jnp.int32, (n, n), 0)
    col = lax.broadcasted_iota(jnp.int32, (n, n), 1)
    eye = jnp.where(row == col, 1.0, 0.0).astype(F32)
    first = ((row ^ col) < 2) & ((row & 1) != 0) & ((col & 1) == 0)
    ts = [eye - jnp.where(first, l, 0.0) for l in lmats]
    s = 2
    while s < block:
        lower_left = ((row ^ col) < 2 * s) & ((row & s) != 0) & ((col & s) == 0)
        us = [_mm3(jnp.where(lower_left, l, 0.0), t, _dot) for l, t in zip(lmats, ts)]
        ts = [t - _mm3(t, u, _dot) for t, u in zip(ts, us)]
        s *= 2
    return ts


def _rwkv_kernel(p_ref, sh0_ref, s0_ref, mu_ref, w0_ref, w2_ref, a0_ref, a2_ref, g2_ref,
                 kk_ref, ka_ref, rk_ref, lw_ref, lb_ref, bd_ref,
                 o_ref, s_out_ref, s_ref, prev_ref, *, c, t_valid, width):
    ci = pl.program_id(1)
    npairs = width // LANES

    @pl.when(ci == 0)
    def _():
        s_ref[...] = s0_ref[0]
        prev_ref[0:1, :] = sh0_ref[0]

    p = p_ref[0]
    row1 = lax.broadcasted_iota(jnp.int32, (c, 1), 0)
    prev = jnp.where(row1 == 0, prev_ref[0:1, :], pltpu.roll(p, 1, axis=0))
    prev_ref[0:1, :] = p[c - 1:c, :]
    xs = p + mu_ref[...] * (prev - p)
    r = xs[:, 0:width]
    k = xs[:, width:2 * width]
    v = xs[:, 2 * width:3 * width]
    dwa = xs[:, 3 * width:3 * width + LANES]
    dg = xs[:, 3 * width + LANES:3 * width + 2 * LANES]

    u = w0_ref[...] + _dot(jnp.tanh(dwa).astype(BF16), w2_ref[...])
    logw = -EXP_M05 * jax.nn.sigmoid(u)
    a = jax.nn.sigmoid(a0_ref[...] + _dot(dwa.astype(BF16), a2_ref[...]))
    g = _dot(jax.nn.sigmoid(dg).astype(BF16), g2_ref[...])

    bd = bd_ref[...]
    kkr = k * kk_ref[...]
    kk = kkr / jnp.maximum(jnp.sqrt(_dot_xw(kkr * kkr, bd)), 1e-12)
    kmod = k * (1.0 + (a - 1.0) * ka_ref[...])
    if t_valid % c != 0:
        valid = ((ci * c + row1) < t_valid).astype(F32)
        logw = logw * valid
        kk = kk * valid
        kmod = kmod * valid
        v = v * valid
    b = kk * a

    rr = lax.broadcasted_iota(jnp.int32, (c, c), 0)
    cc = lax.broadcasted_iota(jnp.int32, (c, c), 1)
    tril = jnp.where(cc <= rr, 1.0, 0.0).astype(BF16)
    lc = _dot_wx(tril, logw, parts=3)
    lc_end = lc[c - 1:c, :]
    c_in = jnp.exp(lc)
    inv_c = jnp.exp(-lc)
    to_end = jnp.exp(lc_end - lc)
    kk_t = kk * jnp.exp(lc - logw)
    b_t = b * inv_c
    k_t = kmod * inv_c
    r_t = r * c_in
    b_end = b * to_end
    k_end = kmod * to_end
    c_end = jnp.exp(lc_end)

    n2 = 2 * c
    lane = lax.broadcasted_iota(jnp.int32, (1, LANES), 1)
    m0 = (lane < RW_HD).astype(F32)
    m1 = 1.0 - m0
    r2 = lax.broadcasted_iota(jnp.int32, (n2, n2), 0)
    c2 = lax.broadcasted_iota(jnp.int32, (n2, n2), 1)
    strict = c2 < r2
    incl = c2 <= r2

    def stack(z):
        return jnp.concatenate([z * m0, z * m1], axis=0)

    pairs = range(npairs)
    sls = [slice(pi * LANES, (pi + 1) * LANES) for pi in pairs]
    kks = [stack(kk_t[:, sl]) for sl in sls]
    bs = [stack(b_t[:, sl]) for sl in sls]
    ks = [stack(k_t[:, sl]) for sl in sls]
    rs = [stack(r_t[:, sl]) for sl in sls]
    vs = [stack(v[:, sl]) for sl in sls]
    s_old = [s_ref[pi] for pi in pairs]
    a_kb = [jnp.where(strict, _mm3(kks[i], bs[i], _dot_nt), 0.0) for i in pairs]
    a_kk = [jnp.where(strict, _mm3(kks[i], ks[i], _dot_nt), 0.0) for i in pairs]
    rhs = [-(_mm3(kks[i], s_old[i], _dot_nt) + _mm3(a_kk[i], vs[i], _dot)) for i in pairs]
    tinv = _unit_lower_inverse(a_kb, n2, c)
    sas = [_mm3(tinv[i], rhs[i], _dot) for i in pairs]
    a_rb = [jnp.where(incl, _mm1(rs[i], bs[i], _dot_nt), 0.0) for i in pairs]
    a_rk = [jnp.where(incl, _mm1(rs[i], ks[i], _dot_nt), 0.0) for i in pairs]
    y2 = [_mm1(rs[i], s_old[i], _dot_nt) + _mm1(a_rb[i], sas[i], _dot) + _mm1(a_rk[i], vs[i], _dot)
          for i in pairs]
    for i in pairs:
        s_ref[i] = (s_old[i] * c_end[:, sls[i]] + _mm1(sas[i], stack(b_end[:, sls[i]]), _dot_tn)
                    + _mm1(vs[i], stack(k_end[:, sls[i]]), _dot_tn))
    y = jnp.concatenate([z[0:c] + z[c:n2] for z in y2], axis=1)

    inv_n = 1.0 / RW_HD
    mean = _dot_xw(y, bd) * inv_n
    d = y - mean
    var = _dot_xw(d * d, bd) * inv_n
    yn = d * lax.rsqrt(var + RW_LN_EPS) * lw_ref[...] + lb_ref[...]
    bonus = _dot_xw(r * kmod * rk_ref[...], bd) * v
    o_ref[0] = (yn + bonus) * g

    @pl.when(ci == pl.num_programs(1) - 1)
    def _():
        s_out_ref[0] = s_ref[...]


def _rwkv_mix(p_rw, shift0, s0_pairs, prm, c, t_valid):
    b, tp, cols = p_rw.shape
    width = prm["w0"].shape[1]
    npairs = width // LANES
    assert tp % c == 0 and cols == 3 * width + 2 * LANES
    vec = lambda n: pl.BlockSpec((1, n), lambda i, j: (0, 0))
    mat = lambda r, n: pl.BlockSpec((r, n), lambda i, j: (0, 0))
    return pl.pallas_call(
        functools.partial(_rwkv_kernel, c=c, t_valid=t_valid, width=width),
        grid=(b, tp // c),
        in_specs=[pl.BlockSpec((1, c, cols), lambda i, j: (i, j, 0)),
                  pl.BlockSpec((1, 1, cols), lambda i, j: (i, 0, 0)),
                  pl.BlockSpec((1, npairs, LANES, LANES), lambda i, j: (i, 0, 0, 0)),
                  vec(cols), vec(width), mat(LANES, width), vec(width), mat(LANES, width),
                  mat(LANES, width), vec(width), vec(width), vec(width), vec(width), vec(width),
                  mat(width, width)],
        out_specs=[pl.BlockSpec((1, c, width), lambda i, j: (i, j, 0)),
                   pl.BlockSpec((1, npairs, LANES, LANES), lambda i, j: (i, 0, 0, 0))],
        out_shape=[jax.ShapeDtypeStruct((b, tp, width), F32),
                   jax.ShapeDtypeStruct((b, npairs, LANES, LANES), F32)],
        scratch_shapes=[pltpu.VMEM((npairs, LANES, LANES), F32),
                        pltpu.VMEM((SUBLANES, cols), F32)],
        compiler_params=_cparams(("parallel", "arbitrary"), 32),
        name="rwkv_mix",
    )(p_rw, shift0, s0_pairs, prm["mu"], prm["w0"], prm["w2"], prm["a0"], prm["a2"], prm["g2"],
      prm["kk"], prm["ka"], prm["rk"], prm["lnw"], prm["lnb"], prm["bd"])


def _lru_kernel(px_ref, pg_ref, conv0_ref, h0_ref, cw_ref, cb_ref, wa_ref, ba_ref, wx_ref, bx_ref,
                lam_ref, o_ref, hl_ref, ext_ref, h_ref, *, c, t_valid, width):
    ci = pl.program_id(1)
    npairs = width // LANES

    @pl.when(ci == 0)
    def _():
        ext_ref[0:SUBLANES, :] = conv0_ref[0]
        h_ref[0:1, :] = h0_ref[0]

    px = px_ref[0]
    ext_ref[SUBLANES:SUBLANES + c, :] = px
    acc = None
    for j in range(CONV_W - 1):
        term = cw_ref[j:j + 1, :] * ext_ref[SUBLANES - (CONV_W - 1) + j:SUBLANES - (CONV_W - 1) + j + c, :]
        acc = term if acc is None else acc + term
    xc = cb_ref[...] + (acc + cw_ref[CONV_W - 1:CONV_W, :] * px)
    ext_ref[0:SUBLANES, :] = ext_ref[c:c + SUBLANES, :]

    xcb = xc.astype(BF16)
    ga, gx = [], []
    for pi in range(npairs):
        sl = slice(pi * LANES, (pi + 1) * LANES)
        ga.append(_dot(xcb[:, sl], wa_ref[pi]))
        gx.append(_dot(xcb[:, sl], wx_ref[pi]))
    gate_r = jax.nn.sigmoid(jnp.concatenate(ga, axis=1) + ba_ref[...])
    gate_i = jax.nn.sigmoid(jnp.concatenate(gx, axis=1) + bx_ref[...])
    log_a = (-LRU_C * gate_r) * _softplus(-lam_ref[...])
    a_t = jnp.exp(log_a)
    b_t = jnp.sqrt(-jnp.tanh(log_a) * (a_t * a_t + 1.0)) * (gate_i * xc)

    row = lax.broadcasted_iota(jnp.int32, (c, 1), 0)
    b_t = b_t + jnp.where(row == 0, a_t * h_ref[0:1, :], 0.0)
    d = 1
    while d < c:
        keep = row >= d
        a_sh = jnp.where(keep, pltpu.roll(a_t, d, axis=0), 1.0)
        b_sh = jnp.where(keep, pltpu.roll(b_t, d, axis=0), 0.0)
        b_t = a_t * b_sh + b_t
        a_t = a_t * a_sh
        d *= 2
    hs = b_t
    h_ref[0:1, :] = hs[c - 1:c, :]
    o_ref[0] = hs * _gelu_tanh(pg_ref[0])

    last = t_valid - 1

    @pl.when(ci == last // c)
    def _():
        hl_ref[0] = hs[last % c:last % c + 1, :]


def _lru_mix(p_x, p_gate, conv0_pad, h0, prm, c, t_valid):
    b, tp, width = p_x.shape
    npairs = width // LANES
    assert tp % c == 0 and c >= SUBLANES
    vec = lambda n: pl.BlockSpec((1, n), lambda i, j: (0, 0))
    blk = pl.BlockSpec((npairs, LANES, LANES), lambda i, j: (0, 0, 0))
    return pl.pallas_call(
        functools.partial(_lru_kernel, c=c, t_valid=t_valid, width=width),
        grid=(b, tp // c),
        in_specs=[pl.BlockSpec((1, c, width), lambda i, j: (i, j, 0)),
                  pl.BlockSpec((1, c, width), lambda i, j: (i, j, 0)),
                  pl.BlockSpec((1, SUBLANES, width), lambda i, j: (i, 0, 0)),
                  pl.BlockSpec((1, 1, width), lambda i, j: (i, 0, 0)),
                  pl.BlockSpec((CONV_W, width), lambda i, j: (0, 0)),
                  vec(width), blk, vec(width), blk, vec(width), vec(width)],
        out_specs=[pl.BlockSpec((1, c, width), lambda i, j: (i, j, 0)),
                   pl.BlockSpec((1, 1, width), lambda i, j: (i, 0, 0))],
        out_shape=[jax.ShapeDtypeStruct((b, tp, width), F32),
                   jax.ShapeDtypeStruct((b, 1, width), F32)],
        scratch_shapes=[pltpu.VMEM((c + SUBLANES, width), F32),
                        pltpu.VMEM((SUBLANES, width), F32)],
        compiler_params=_cparams(("parallel", "arbitrary"), 32),
        name="lru_mix",
    )(p_x, p_gate, conv0_pad, h0, prm["cw"], prm["cb"], prm["wa"], prm["ba"], prm["wx"], prm["bx"],
      prm["lam"])


def _sb_weights(zs, cums, upper2, masks):
    sps = [jnp.maximum(z, 0.0) + jnp.log(1.0 + jnp.exp(-jnp.abs(z))) for z in zs]
    sps = [sp if m is None else jnp.where(m, sp, 0.0) for sp, m in zip(sps, masks)]
    suffix = [_dot(jnp.concatenate(_split2(sp), axis=1), upper2) for sp in sps]
    atts = [jnp.exp((z - c) - sf) for z, c, sf in zip(zs, cums, suffix)]
    atts = [a if m is None else jnp.where(m, a, 0.0) for a, m in zip(atts, masks)]
    cums = [c + jnp.sum(sp, axis=-1, keepdims=True) for c, sp in zip(cums, sps)]
    return atts, cums


def _upper2(n):
    jj = lax.broadcasted_iota(jnp.int32, (2 * n, n), 0) & (n - 1)
    ss = lax.broadcasted_iota(jnp.int32, (2 * n, n), 1)
    return jnp.where(jj >= ss, 1.0, 0.0).astype(BF16)


def _sb_prompt_kernel(bias_ref, q_ref, k_ref, v_ref, o_ref):
    pi = pl.program_id(1)
    qi = pl.program_id(2)
    blk = SB_TILE
    lane = lax.broadcasted_iota(jnp.int32, (1, LANES), 1)
    first = lane < SB_HD
    q = q_ref[0] * (SB_HD ** -0.5)
    qs = [jnp.where(first, q, 0.0).astype(BF16), jnp.where(first, 0.0, q).astype(BF16)]
    biases = [bias_ref[2 * pi], bias_ref[2 * pi + 1]]
    upper2 = _upper2(blk)

    def block(kb, cums, mask):
        start = pl.multiple_of(kb * blk, blk)
        kblk = k_ref[0, pl.ds(start, blk), :].astype(BF16)
        vblk = v_ref[0, pl.ds(start, blk), :].astype(BF16)
        zs = [_dot_nt(qh, kblk) + bh for qh, bh in zip(qs, biases)]
        atts, cums = _sb_weights(zs, cums, upper2, [mask, mask])
        return [_dot(a.astype(BF16), vblk) for a in atts], cums

    qpos = lax.broadcasted_iota(jnp.int32, (blk, blk), 0)
    kpos = lax.broadcasted_iota(jnp.int32, (blk, blk), 1)
    zero = jnp.zeros((blk, 1), F32)
    accs, cums = block(qi, [zero, zero], kpos < qpos)

    def body(i, carry):
        accs, cums = carry
        pvs, cums = block(qi - 1 - i, cums, None)
        return [a + p for a, p in zip(accs, pvs)], cums

    accs, cums = lax.fori_loop(0, qi, body, (accs, cums))
    o_ref[0] = jnp.where(first, accs[0], accs[1])


def _sb_prompt(qkv, bias, heads):
    b, t, w3 = qkv.shape
    w = w3 // 3
    npairs = w // LANES
    assert t % SB_TILE == 0 and heads * SB_HD == w
    return pl.pallas_call(
        _sb_prompt_kernel,
        grid=(b, npairs, t // SB_TILE),
        in_specs=[pl.BlockSpec(memory_space=pltpu.SMEM),
                  pl.BlockSpec((1, SB_TILE, LANES), lambda i, p, j: (i, j, p)),
                  pl.BlockSpec((1, t, LANES), lambda i, p, j: (i, 0, npairs + p)),
                  pl.BlockSpec((1, t, LANES), lambda i, p, j: (i, 0, 2 * npairs + p))],
        out_specs=pl.BlockSpec((1, SB_TILE, LANES), lambda i, p, j: (i, j, p)),
        out_shape=jax.ShapeDtypeStruct((b, t, w), F32),
        compiler_params=_cparams(("parallel", "parallel", "arbitrary"), 32),
        name="sb_prompt",
    )(bias, qkv, qkv, qkv)


def _sb_decode_kernel(pt_ref, bias_ref, q_ref, *refs, pages_per_step, heads):
    k_refs = refs[:pages_per_step]
    v_refs = refs[pages_per_step:2 * pages_per_step]
    o_ref, acc_ref, cum_ref = refs[2 * pages_per_step:]
    j = pl.program_id(1)
    w = heads * SB_HD

    @pl.when(j == 0)
    def _():
        acc_ref[...] = jnp.zeros_like(acc_ref)
        cum_ref[...] = jnp.zeros_like(cum_ref)

    hrow = lax.broadcasted_iota(jnp.int32, (heads, w), 0)
    hcol = lax.broadcasted_iota(jnp.int32, (heads, w), 1) >> (SB_HD.bit_length() - 1)
    own = hrow == hcol
    qm = jnp.where(own, q_ref[0] * (SB_HD ** -0.5), 0.0).astype(BF16)
    upper2 = _upper2(PAGE)
    bias = bias_ref[...]

    pages = range(pages_per_step)
    zs = [_dot(qm, k_refs[i][...].astype(BF16)) + bias for i in pages]
    none = [None] * pages_per_step
    zero = jnp.zeros((heads, 1), F32)
    atts, tots = _sb_weights(zs, [zero] * pages_per_step, upper2, none)
    acc = acc_ref[...]
    cum = cum_ref[...]
    for i in pages:
        att = atts[i] * jnp.exp(-cum)
        acc = acc + _dot_nt(att.astype(BF16), v_refs[i][...].astype(BF16))
        cum = cum + tots[i]
    acc_ref[...] = acc
    cum_ref[...] = cum

    @pl.when(j == pl.num_programs(1) - 1)
    def _():
        o_ref[0] = jnp.sum(jnp.where(own, acc, 0.0), axis=0, keepdims=True)


def _sb_decode(q, k_t, v_t, page_table, bias, heads, pages_per_step):
    b, _, w = q.shape
    n_pages = page_table.shape[1]
    assert n_pages % pages_per_step == 0 and w == heads * SB_HD and k_t.shape[1] == PAGE
    steps = n_pages // pages_per_step

    def page_spec(i):
        return pl.BlockSpec((w, PAGE),
                            lambda bi, j, pt: (pt[bi, n_pages - 1 - (j * pages_per_step + i)], 0))

    return pl.pallas_call(
        functools.partial(_sb_decode_kernel, pages_per_step=pages_per_step, heads=heads),
        grid_spec=pltpu.PrefetchScalarGridSpec(
            num_scalar_prefetch=1,
            grid=(b, steps),
            in_specs=[pl.BlockSpec((heads, 1), lambda bi, j, pt: (0, 0)),
                      pl.BlockSpec((1, 1, w), lambda bi, j, pt: (bi, 0, 0))]
                     + [page_spec(i) for i in range(pages_per_step)] * 2,
            out_specs=pl.BlockSpec((1, 1, w), lambda bi, j, pt: (bi, 0, 0)),
            scratch_shapes=[pltpu.VMEM((heads, w), F32), pltpu.VMEM((heads, 1), F32)]),
        out_shape=jax.ShapeDtypeStruct((b, 1, w), F32),
        compiler_params=_cparams(("parallel", "arbitrary"), 48),
        name="sb_decode",
    )(page_table, bias.reshape(heads, 1), q, *([k_t] * pages_per_step), *([v_t] * pages_per_step))


def _pair_blockdiag(w):
    n, d, _ = w.shape
    w = w.reshape(n // 2, 2, d, d)
    z = jnp.zeros_like(w[:, 0])
    top = jnp.concatenate([w[:, 0], z], axis=2)
    bot = jnp.concatenate([z, w[:, 1]], axis=2)
    return jnp.concatenate([top, bot], axis=1)


def _state_to_pairs(s):
    b, h, n, _ = s.shape
    return _pair_blockdiag(s.reshape(b * h, n, n)).reshape(b, h // 2, 2 * n, 2 * n)


def _pairs_to_state(sp, n):
    b, hp = sp.shape[0], sp.shape[1]
    return jnp.stack([sp[:, :, :n, :n], sp[:, :, n:, n:]], axis=2).reshape(b, 2 * hp, n, n)


def _pad_rows(x, multiple):
    t = x.shape[1]
    t_pad = -(-t // multiple) * multiple
    return x if t_pad == t else jnp.pad(x, ((0, 0), (0, t_pad - t), (0, 0)))


def kernel(x_prompt, x_sample, state_rwkv_wkv, state_rwkv_shift, state_lru_h, state_lru_conv,
           cache_sb_k, cache_sb_v, page_table, cache_mem_k, cache_mem_v, mem_prompt,
           g_mix, g_mem, g_memkv, g_ffn, wq_mem, wk_mem, wv_mem, wo_mem, qn_mem, kn_mem,
           w_ffn_gate, w_ffn_up, w_ffn_down, w_in_ab, mu_shift, rw_w0, rw_w2, rw_a0, rw_a2,
           rw_g2, rw_kk, rw_ka, rw_rk, rw_lnx_w, rw_lnx_b, lru_conv_w, lru_conv_b, lru_wa,
           lru_ba, lru_wx, lru_bx, lru_lambda, w_out_ab, w_qkv_sb, w_out_sb, sb_bias):
    depth, d_model = g_mix.shape
    mem_heads = cache_mem_k.shape[3]
    mem_w = wq_mem.shape[2]
    rw_w = rw_w0.shape[1]
    rw_cols = mu_shift.shape[1]
    lru_w = lru_lambda.shape[1]
    rw_heads = rw_w // RW_HD
    sb_heads = sb_bias.shape[1]
    sb_w = sb_heads * SB_HD
    d_ff = w_ffn_gate.shape[2]
    lora_w = rw_w2.shape[1]
    lora_a = rw_a2.shape[1]
    assert lora_w + lora_a == LANES and rw_g2.shape[1] == LANES
    tf = 256
    assert d_ff % tf == 0
    bf = lambda z: z.astype(BF16)

    wg3 = [bf(w_ffn_gate[i]).reshape(d_model, d_ff // tf, tf).transpose(1, 0, 2) for i in range(depth)]
    wu3 = [bf(w_ffn_up[i]).reshape(d_model, d_ff // tf, tf).transpose(1, 0, 2) for i in range(depth)]
    wd3 = [bf(w_ffn_down[i]).reshape(d_ff // tf, tf, d_model) for i in range(depth)]
    head_id = jnp.arange(rw_w) // RW_HD
    bd = (head_id[:, None] == head_id[None, :]).astype(BF16)

    def even_params(e):
        zw = jnp.zeros((lora_a, rw_w), F32)
        za = jnp.zeros((lora_w, rw_w), F32)
        row = lambda z: z.reshape(1, -1)
        rw = dict(mu=row(mu_shift[e]), w0=row(rw_w0[e]), w2=bf(jnp.concatenate([rw_w2[e], zw], axis=0)),
                  a0=row(rw_a0[e]), a2=bf(jnp.concatenate([za, rw_a2[e]], axis=0)), g2=bf(rw_g2[e]),
                  kk=row(rw_kk[e]), ka=row(rw_ka[e]), rk=row(rw_rk[e]), lnw=row(rw_lnx_w[e]),
                  lnb=row(rw_lnx_b[e]), bd=bd)
        lru = dict(cw=lru_conv_w[e], cb=row(lru_conv_b[e]), wa=bf(_pair_blockdiag(lru_wa[e])),
                   ba=row(lru_ba[e]), wx=bf(_pair_blockdiag(lru_wx[e])), bx=row(lru_bx[e]),
                   lam=row(lru_lambda[e]))
        return rw, lru

    def trunk(x, sample):
        bsz, t, _ = x.shape
        m = bsz * t
        tm = 512 if m % 512 == 0 else m
        out = {}
        xf = x.reshape(m, d_model)
        for i in range(depth):
            if i % 2 == 0:
                e = i // 2
                rw, lru = even_params(e)
                p_rw, p_x, p_gate = _norm_proj(xf, g_mix[i], bf(w_in_ab[e]),
                                               [rw_cols, rw_cols + lru_w], tm)
                p_rw = p_rw.reshape(bsz, t, rw_cols)
                p_x = p_x.reshape(bsz, t, lru_w)
                p_gate = p_gate.reshape(bsz, t, lru_w)
                if sample:
                    wkv0, shift0 = state_rwkv_wkv[e], state_rwkv_shift[e]
                    h0, conv0 = state_lru_h[e], state_lru_conv[e]
                else:
                    wkv0 = jnp.zeros((bsz, rw_heads, RW_HD, RW_HD), F32)
                    shift0 = jnp.zeros((bsz, rw_cols), F32)
                    h0 = jnp.zeros((bsz, lru_w), F32)
                    conv0 = jnp.zeros((bsz, CONV_W - 1, lru_w), F32)
                o_rw, s_pairs = _rwkv_mix(_pad_rows(p_rw, RW_CHUNK), shift0.reshape(bsz, 1, rw_cols),
                                          _state_to_pairs(wkv0), rw, RW_CHUNK, t)
                conv0_pad = jnp.pad(conv0, ((0, 0), (SUBLANES - (CONV_W - 1), 0), (0, 0)))
                px_pad = _pad_rows(p_x, BF16_ROWS)
                c_lru = LRU_CHUNK if px_pad.shape[1] % LRU_CHUNK == 0 else px_pad.shape[1]
                o_lru, h_last = _lru_mix(px_pad, _pad_rows(p_gate, BF16_ROWS), conv0_pad,
                                         h0.reshape(bsz, 1, lru_w), lru, c_lru, t)
                out.setdefault("wkv", []).append(_pairs_to_state(s_pairs, RW_HD))
                out.setdefault("shift", []).append(p_rw[:, t - 1])
                out.setdefault("lru_h", []).append(h_last[:, 0])
                out.setdefault("lru_conv", []).append(
                    jnp.concatenate([conv0, p_x], axis=1)[:, -(CONV_W - 1):])
                w_out = bf(w_out_ab[e])
                xf = _proj_residual(xf, [o_rw[:, :t].reshape(m, rw_w), o_lru[:, :t].reshape(m, lru_w)],
                                    [w_out[:rw_w], w_out[rw_w:]], tm)
            else:
                o = i // 2
                (qkv,) = _norm_proj(xf, g_mix[i], bf(w_qkv_sb[o]), [], tm)
                qkv = qkv.reshape(bsz, t, 3 * sb_w)
                if sample:
                    pool = cache_sb_k.shape[1]
                    k_t = jnp.transpose(cache_sb_k, (0, 1, 3, 4, 2)).reshape(-1, PAGE)
                    v_t = jnp.transpose(cache_sb_v, (0, 1, 3, 4, 2)).reshape(-1, PAGE)
                    att = _sb_decode(qkv[:, :, :sb_w], k_t, v_t, page_table + o * pool, sb_bias[o],
                                     sb_heads, 4)
                else:
                    att = _sb_prompt(qkv, sb_bias[o], sb_heads)
                out.setdefault("sb_k", []).append(qkv[:, :, sb_w:2 * sb_w].reshape(bsz, t, sb_heads, SB_HD))
                out.setdefault("sb_v", []).append(qkv[:, :, 2 * sb_w:].reshape(bsz, t, sb_heads, SB_HD))
                xf = _proj_residual(xf, [att.reshape(m, sb_w)], [bf(w_out_sb[o])], tm)
            if sample:
                mk = cache_mem_k[i].reshape(bsz, -1, mem_w)
                mv = cache_mem_v[i].reshape(bsz, -1, mem_w)
            else:
                mk, mv = _memory_kv(mem_prompt, g_memkv[i], bf(wk_mem[i]), bf(wv_mem[i]), kn_mem[i],
                                    mem_heads)
                out.setdefault("mem_k", []).append(mk.reshape(bsz, -1, mem_heads, mem_w // mem_heads))
                out.setdefault("mem_v", []).append(mv.reshape(bsz, -1, mem_heads, mem_w // mem_heads))
            x3 = _pad_rows(xf.reshape(bsz, t, d_model), BF16_ROWS)
            x3 = _memx_residual(x3, g_mem[i], bf(wq_mem[i]), qn_mem[i], mk, mv, bf(wo_mem[i]),
                                mem_heads, 512 if x3.shape[1] % 512 == 0 else x3.shape[1])
            xf = x3[:, :t].reshape(m, d_model)
            xf = _ffn_residual(xf, g_ffn[i], wg3[i], wu3[i], wd3[i], tm)
        return xf.reshape(bsz, t, d_model), out

    y_p, sp = trunk(x_prompt, False)
    y_s, ss = trunk(x_sample, True)
    st = lambda arrs: jnp.stack(arrs, axis=0)
    return (y_p, y_s, st(sp["wkv"]), st(ss["wkv"]), st(sp["shift"]), st(ss["shift"]),
            st(sp["lru_h"]), st(ss["lru_h"]), st(sp["lru_conv"]), st(ss["lru_conv"]),
            st(sp["sb_k"]), st(ss["sb_k"]), st(sp["sb_v"]), st(ss["sb_v"]),
            st(sp["mem_k"]), st(sp["mem_v"]))
```

```python
import functools
import math

import jax
import jax.numpy as jnp
from jax import lax
from jax.experimental import pallas as pl
from jax.experimental.pallas import tpu as pltpu

F32 = jnp.float32
BF16 = jnp.bfloat16

NORM_EPS = 1e-6
RW_HD = 64
LANES = 128
SUBLANES = 8
BF16_ROWS = 16
RW_CHUNK = 64
RW_ROWS_PER_STEP = 2
LRU_CHUNK = 256
RW_LN_EPS = RW_HD * 1e-5
LRU_C = 8.0
LRU_BD = 64
CONV_W = 4
SB_HD = 64
PAGE = 128
SB_TILE = 256
EXP_M05 = math.exp(-0.5)
VMEM_BYTES_V7X = 64 * 1024 * 1024


def _cparams(sem, vmem_mb):
    assert vmem_mb * 1024 * 1024 < VMEM_BYTES_V7X
    return pltpu.CompilerParams(dimension_semantics=sem, vmem_limit_bytes=vmem_mb * 1024 * 1024)


def _dot(a, b):
    return jnp.dot(a, b, preferred_element_type=F32)


def _dot_nt(a, b):
    return lax.dot_general(a, b, (((1,), (1,)), ((), ())), preferred_element_type=F32)


def _dot_tn(a, b):
    return lax.dot_general(a, b, (((0,), (0,)), ((), ())), preferred_element_type=F32)


def _split2(x):
    hi = x.astype(BF16)
    lo = (x - hi.astype(F32)).astype(BF16)
    return hi, lo


def _split3(x):
    hi = x.astype(BF16)
    r = x - hi.astype(F32)
    mid = r.astype(BF16)
    lo = (r - mid.astype(F32)).astype(BF16)
    return hi, mid, lo


def _dot_xw(x, w_bf16, parts=2):
    pieces = _split2(x) if parts == 2 else _split3(x)
    out = _dot(pieces[0], w_bf16)
    for p in pieces[1:]:
        out = out + _dot(p, w_bf16)
    return out


def _dot_wx(w_bf16, x, parts=3):
    pieces = _split2(x) if parts == 2 else _split3(x)
    out = _dot(w_bf16, pieces[0])
    for p in pieces[1:]:
        out = out + _dot(w_bf16, p)
    return out


def _mm3(a, b, dot):
    ah, al = _split2(a)
    bh, bl = _split2(b)
    return dot(ah, bh) + (dot(al, bh) + dot(ah, bl))


def _mm1(a, b, dot):
    return dot(a.astype(BF16), b.astype(BF16))


def _softplus(x):
    return jnp.maximum(x, 0.0) + jnp.log1p(jnp.exp(-jnp.abs(x)))


def _gelu_tanh(x):
    return 0.5 * x * (1.0 + jnp.tanh(math.sqrt(2.0 / math.pi) * (x + 0.044715 * (x * x * x))))


def _rms(x, g, eps=NORM_EPS):
    ms = jnp.mean(x * x, axis=-1, keepdims=True)
    return x * lax.rsqrt(ms + eps) * g


def _norm_proj_kernel(x_ref, g_ref, w_ref, *o_refs, splits, chunk):
    h = _rms(x_ref[...], g_ref[...]).astype(BF16)
    for o_ref, (s, e) in zip(o_refs, splits):
        for c0 in range(s, e, chunk):
            c1 = min(c0 + chunk, e)
            o_ref[:, c0 - s:c1 - s] = _dot(h, w_ref[:, c0:c1])


def _norm_proj(x, g, w_bf16, splits, tm):
    m, d = x.shape
    n = w_bf16.shape[1]
    assert m % tm == 0
    bounds = [0] + list(splits) + [n]
    ranges = [(bounds[i], bounds[i + 1]) for i in range(len(bounds) - 1)]
    out_shape = [jax.ShapeDtypeStruct((m, e - s), F32) for s, e in ranges]
    out_specs = [pl.BlockSpec((tm, e - s), lambda i: (i, 0)) for s, e in ranges]
    return pl.pallas_call(
        functools.partial(_norm_proj_kernel, splits=ranges, chunk=512),
        grid=(m // tm,),
        in_specs=[pl.BlockSpec((tm, d), lambda i: (i, 0)),
                  pl.BlockSpec((1, d), lambda i: (0, 0)),
                  pl.BlockSpec((d, n), lambda i: (0, 0))],
        out_specs=out_specs,
        out_shape=out_shape,
        compiler_params=_cparams(("parallel",), 48),
        name="norm_proj",
    )(x, g.reshape(1, d), w_bf16)


def _proj_res_kernel(*refs, n):
    x_ref, a_refs, w_refs, o_ref = refs[0], refs[1:1 + n], refs[1 + n:1 + 2 * n], refs[-1]
    acc = x_ref[...]
    for a_ref, w_ref in zip(a_refs, w_refs):
        acc = acc + _dot(a_ref[...].astype(BF16), w_ref[...])
    o_ref[...] = acc


def _proj_residual(x, a_list, w_list, tm):
    m, d = x.shape
    n = len(a_list)
    assert m % tm == 0
    in_specs = [pl.BlockSpec((tm, d), lambda i: (i, 0))]
    in_specs += [pl.BlockSpec((tm, a.shape[1]), lambda i: (i, 0)) for a in a_list]
    in_specs += [pl.BlockSpec(w.shape, lambda i: (0, 0)) for w in w_list]
    return pl.pallas_call(
        functools.partial(_proj_res_kernel, n=n),
        grid=(m // tm,),
        in_specs=in_specs,
        out_specs=pl.BlockSpec((tm, d), lambda i: (i, 0)),
        out_shape=jax.ShapeDtypeStruct((m, d), F32),
        compiler_params=_cparams(("parallel",), 32),
        name="proj_residual",
    )(x, *a_list, *w_list)


def _ffn_kernel(x_ref, g_ref, wg_ref, wu_ref, wd_ref, o_ref, h_ref):
    x = x_ref[...]
    h_ref[...] = _rms(x, g_ref[...]).astype(BF16)
    o_ref[...] = x

    def body(j, carry):
        h = h_ref[...]
        gate = _dot(h, wg_ref[j])
        up = _dot(h, wu_ref[j])
        act = (gate * jax.nn.sigmoid(gate) * up).astype(BF16)
        o_ref[...] += _dot(act, wd_ref[j])
        return carry

    lax.fori_loop(0, wg_ref.shape[0], body, 0)


def _ffn_residual(x, g, wg3, wu3, wd3, tm):
    m, d = x.shape
    nf, _, tf = wg3.shape
    assert m % tm == 0
    return pl.pallas_call(
        _ffn_kernel,
        grid=(m // tm,),
        in_specs=[pl.BlockSpec((tm, d), lambda i: (i, 0)),
                  pl.BlockSpec((1, d), lambda i: (0, 0)),
                  pl.BlockSpec((nf, d, tf), lambda i: (0, 0, 0)),
                  pl.BlockSpec((nf, d, tf), lambda i: (0, 0, 0)),
                  pl.BlockSpec((nf, tf, d), lambda i: (0, 0, 0))],
        out_specs=pl.BlockSpec((tm, d), lambda i: (i, 0)),
        out_shape=jax.ShapeDtypeStruct((m, d), F32),
        scratch_shapes=[pltpu.VMEM((tm, d), BF16)],
        compiler_params=_cparams(("parallel",), 56),
        name="ffn_residual",
    )(x, g.reshape(1, d), wg3, wu3, wd3)


def _memkv_kernel(m_ref, g_ref, wk_ref, wv_ref, kn_ref, k_ref, v_ref, *, heads, hd):
    mn = _rms(m_ref[0], g_ref[...]).astype(BF16)
    k = _dot(mn, wk_ref[...])
    v_ref[0] = _dot(mn, wv_ref[...])
    for hh in range(heads):
        sl = slice(hh * hd, (hh + 1) * hd)
        k_ref[0, :, sl] = _rms(k[:, sl], kn_ref[...])


def _memory_kv(mem, g, wk, wv, kn, heads):
    b, nm, d = mem.shape
    w = wk.shape[1]
    hd = w // heads
    return pl.pallas_call(
        functools.partial(_memkv_kernel, heads=heads, hd=hd),
        grid=(b,),
        in_specs=[pl.BlockSpec((1, nm, d), lambda i: (i, 0, 0)),
                  pl.BlockSpec((1, d), lambda i: (0, 0)),
                  pl.BlockSpec((d, w), lambda i: (0, 0)),
                  pl.BlockSpec((d, w), lambda i: (0, 0)),
                  pl.BlockSpec((1, hd), lambda i: (0, 0))],
        out_specs=[pl.BlockSpec((1, nm, w), lambda i: (i, 0, 0)),
                   pl.BlockSpec((1, nm, w), lambda i: (i, 0, 0))],
        out_shape=[jax.ShapeDtypeStruct((b, nm, w), F32)] * 2,
        compiler_params=_cparams(("parallel",), 32),
        name="memory_kv",
    )(mem, g.reshape(1, d), wk, wv, kn.reshape(1, hd))


def _memx_kernel(x_ref, g_ref, wq_ref, qn_ref, k_ref, v_ref, wo_ref, o_ref, *, heads, hd):
    x = x_ref[0]
    h = _rms(x, g_ref[...]).astype(BF16)
    q = _dot(h, wq_ref[...])
    acc = x
    scale = hd ** -0.5
    for hh in range(heads):
        sl = slice(hh * hd, (hh + 1) * hd)
        qh = _rms(q[:, sl], qn_ref[...]).astype(BF16)
        s = _dot_nt(qh, k_ref[0, :, sl].astype(BF16)) * scale
        e = jnp.exp(s - jnp.max(s, axis=-1, keepdims=True))
        p = e / jnp.sum(e, axis=-1, keepdims=True)
        oh = _dot(p.astype(BF16), v_ref[0, :, sl].astype(BF16))
        acc = acc + _dot(oh.astype(BF16), wo_ref[sl, :])
    o_ref[0] = acc


def _memx_residual(x, g, wq, qn, k, v, wo, heads, tm):
    b, t, d = x.shape
    nm, w = k.shape[1], k.shape[2]
    hd = w // heads
    assert t % tm == 0
    return pl.pallas_call(
        functools.partial(_memx_kernel, heads=heads, hd=hd),
        grid=(b, t // tm),
        in_specs=[pl.BlockSpec((1, tm, d), lambda i, j: (i, j, 0)),
                  pl.BlockSpec((1, d), lambda i, j: (0, 0)),
                  pl.BlockSpec((d, w), lambda i, j: (0, 0)),
                  pl.BlockSpec((1, hd), lambda i, j: (0, 0)),
                  pl.BlockSpec((1, nm, w), lambda i, j: (i, 0, 0)),
                  pl.BlockSpec((1, nm, w), lambda i, j: (i, 0, 0)),
                  pl.BlockSpec((w, d), lambda i, j: (0, 0))],
        out_specs=pl.BlockSpec((1, tm, d), lambda i, j: (i, j, 0)),
        out_shape=jax.ShapeDtypeStruct((b, t, d), F32),
        compiler_params=_cparams(("parallel", "parallel"), 32),
        name="memx_residual",
    )(x, g.reshape(1, d), wq, qn.reshape(1, hd), k, v, wo)


def _unit_lower_inverse(lmats, n, block):
    row = lax.broadcasted_iota(jnp.int32, (n, n), 0)
    col = lax.broadcasted_iota(jnp.int32, (n, n), 1)
    eye = jnp.where(row == col, 1.0, 0.0).astype(F32)
    first = ((row ^ col) < 2) & ((row & 1) != 0) & ((col & 1) == 0)
    ts = [eye - jnp.where(first, l, 0.0) for l in lmats]
    s = 2
    while s < block:
        lower_left = ((row ^ col) < 2 * s) & ((row & s) != 0) & ((col & s) == 0)
        us = [_mm1(jnp.where(lower_left, l, 0.0), t, _dot) for l, t in zip(lmats, ts)]
        ts = [t - _mm1(t, u, _dot) for t, u in zip(ts, us)]
        s *= 2
    return ts


def _rwkv_kernel(p_ref, sh0_ref, s0_ref, mu_ref, w0_ref, w2_ref, a0_ref, a2_ref, g2_ref,
                 kk_ref, ka_ref, rk_ref, lw_ref, lb_ref, bd_ref,
                 o_ref, s_out_ref, s_ref, prev_ref, *, c, t_valid, width):
    ci = pl.program_id(1)
    nb = p_ref.shape[0]
    npairs = width // LANES
    sls = [slice(pi * LANES, (pi + 1) * LANES) for pi in range(npairs)]

    @pl.when(ci == 0)
    def _():
        s_ref[...] = s0_ref[...]
        prev_ref[:, 0:1, :] = sh0_ref[...]

    bd = bd_ref[...]

    def head_sum(z):
        rows = jnp.concatenate([z[:, sl] for sl in sls], axis=0)
        tot = _dot_xw(rows, bd)
        return jnp.concatenate([tot[i * c:(i + 1) * c] for i in range(npairs)], axis=1)

    row1 = lax.broadcasted_iota(jnp.int32, (c, 1), 0)
    rr = lax.broadcasted_iota(jnp.int32, (c, c), 0)
    cc = lax.broadcasted_iota(jnp.int32, (c, c), 1)
    tril = jnp.where(cc <= rr, 1.0, 0.0).astype(BF16)

    def prepare(bi):
        p = p_ref[bi]
        prev = jnp.where(row1 == 0, prev_ref[bi, 0:1, :], pltpu.roll(p, 1, axis=0))
        prev_ref[bi, 0:1, :] = p[c - 1:c, :]
        xs = p + mu_ref[...] * (prev - p)
        r = xs[:, 0:width]
        k = xs[:, width:2 * width]
        v = xs[:, 2 * width:3 * width]
        dwa = xs[:, 3 * width:3 * width + LANES]
        dg = xs[:, 3 * width + LANES:3 * width + 2 * LANES]
        u = w0_ref[...] + _dot(jnp.tanh(dwa).astype(BF16), w2_ref[...])
        logw = -EXP_M05 * jax.nn.sigmoid(u)
        a = jax.nn.sigmoid(a0_ref[...] + _dot(dwa.astype(BF16), a2_ref[...]))
        g = _dot(jax.nn.sigmoid(dg).astype(BF16), g2_ref[...])
        kkr = k * kk_ref[...]
        kk = kkr / jnp.maximum(jnp.sqrt(head_sum(kkr * kkr)), 1e-12)
        kmod = k * (1.0 + (a - 1.0) * ka_ref[...])
        if t_valid % c != 0:
            valid = ((ci * c + row1) < t_valid).astype(F32)
            logw, kk, kmod, v = logw * valid, kk * valid, kmod * valid, v * valid
        b = kk * a
        lc = _dot_wx(tril, logw, parts=3)
        lc_end = lc[c - 1:c, :]
        inv_c = jnp.exp(-lc)
        to_end = jnp.exp(lc_end - lc)
        return dict(r=r, v=v, kmod=kmod, g=g, kk_t=kk * jnp.exp(lc - logw), b_t=b * inv_c,
                    k_t=kmod * inv_c, r_t=r * jnp.exp(lc), b_end=b * to_end, k_end=kmod * to_end,
                    c_end=jnp.exp(lc_end))

    preps = [prepare(bi) for bi in range(nb)]

    n2 = 2 * c
    lane = lax.broadcasted_iota(jnp.int32, (1, LANES), 1)
    m0 = (lane < RW_HD).astype(F32)
    m1 = 1.0 - m0
    r2 = lax.broadcasted_iota(jnp.int32, (n2, n2), 0)
    c2 = lax.broadcasted_iota(jnp.int32, (n2, n2), 1)
    strict = c2 < r2
    incl = c2 <= r2

    def stack(z):
        return jnp.concatenate([z * m0, z * m1], axis=0).astype(BF16)

    chains = [(bi, pi) for bi in range(nb) for pi in range(npairs)]
    ids = range(len(chains))
    stk = lambda name: [stack(preps[bi][name][:, sls[pi]]) for bi, pi in chains]
    kks, bs, ks, rs, vs = stk("kk_t"), stk("b_t"), stk("k_t"), stk("r_t"), stk("v")
    s_old = [s_ref[bi, pi] for bi, pi in chains]
    s_bf = [s.astype(BF16) for s in s_old]
    a_kb = [jnp.where(strict, _dot_nt(kks[i], bs[i]), 0.0) for i in ids]
    a_kk = [jnp.where(strict, _dot_nt(kks[i], ks[i]), 0.0).astype(BF16) for i in ids]
    rhs = [-(_dot_nt(kks[i], s_bf[i]) + _dot(a_kk[i], vs[i])) for i in ids]
    tinv = _unit_lower_inverse(a_kb, n2, c)
    sas = [_mm1(tinv[i], rhs[i], _dot).astype(BF16) for i in ids]
    a_rb = [jnp.where(incl, _dot_nt(rs[i], bs[i]), 0.0).astype(BF16) for i in ids]
    a_rk = [jnp.where(incl, _dot_nt(rs[i], ks[i]), 0.0).astype(BF16) for i in ids]
    y2 = [_dot_nt(rs[i], s_bf[i]) + _dot(a_rb[i], sas[i]) + _dot(a_rk[i], vs[i]) for i in ids]
    for i, (bi, pi) in enumerate(chains):
        s_ref[bi, pi] = (s_old[i] * preps[bi]["c_end"][:, sls[pi]]
                         + _dot_tn(sas[i], stack(preps[bi]["b_end"][:, sls[pi]]))
                         + _dot_tn(vs[i], stack(preps[bi]["k_end"][:, sls[pi]])))

    inv_n = 1.0 / RW_HD
    for bi in range(nb):
        pr = preps[bi]
        y = jnp.concatenate([y2[bi * npairs + pi][0:c] + y2[bi * npairs + pi][c:n2]
                             for pi in range(npairs)], axis=1)
        mean = head_sum(y) * inv_n
        d = y - mean
        var = head_sum(d * d) * inv_n
        yn = d * lax.rsqrt(var + RW_LN_EPS) * lw_ref[...] + lb_ref[...]
        bonus = head_sum(pr["r"] * pr["kmod"] * rk_ref[...]) * pr["v"]
        o_ref[bi] = (yn + bonus) * pr["g"]

    @pl.when(ci == pl.num_programs(1) - 1)
    def _():
        s_out_ref[...] = s_ref[...]


def _rwkv_mix(p_rw, shift0, s0_pairs, prm, c, t_valid):
    b, tp, cols = p_rw.shape
    width = prm["w0"].shape[1]
    npairs = width // LANES
    nb = RW_ROWS_PER_STEP if b % RW_ROWS_PER_STEP == 0 else 1
    assert tp % c == 0 and cols == 3 * width + 2 * LANES
    vec = lambda n: pl.BlockSpec((1, n), lambda i, j: (0, 0))
    mat = lambda r, n: pl.BlockSpec((r, n), lambda i, j: (0, 0))
    return pl.pallas_call(
        functools.partial(_rwkv_kernel, c=c, t_valid=t_valid, width=width),
        grid=(b // nb, tp // c),
        in_specs=[pl.BlockSpec((nb, c, cols), lambda i, j: (i, j, 0)),
                  pl.BlockSpec((nb, 1, cols), lambda i, j: (i, 0, 0)),
                  pl.BlockSpec((nb, npairs, LANES, LANES), lambda i, j: (i, 0, 0, 0)),
                  vec(cols), vec(width), mat(LANES, width), vec(width), mat(LANES, width),
                  mat(LANES, width), vec(width), vec(width), vec(width), vec(width), vec(width),
                  mat(LANES, LANES)],
        out_specs=[pl.BlockSpec((nb, c, width), lambda i, j: (i, j, 0)),
                   pl.BlockSpec((nb, npairs, LANES, LANES), lambda i, j: (i, 0, 0, 0))],
        out_shape=[jax.ShapeDtypeStruct((b, tp, width), F32),
                   jax.ShapeDtypeStruct((b, npairs, LANES, LANES), F32)],
        scratch_shapes=[pltpu.VMEM((nb, npairs, LANES, LANES), F32),
                        pltpu.VMEM((nb, SUBLANES, cols), F32)],
        compiler_params=_cparams(("parallel", "arbitrary"), 32),
        name="rwkv_mix",
    )(p_rw, shift0, s0_pairs, prm["mu"], prm["w0"], prm["w2"], prm["a0"], prm["a2"], prm["g2"],
      prm["kk"], prm["ka"], prm["rk"], prm["lnw"], prm["lnb"], prm["bd"])


def _lru_kernel(px_ref, pg_ref, conv0_ref, h0_ref, cw_ref, cb_ref, wa_ref, ba_ref, wx_ref, bx_ref,
                lam_ref, o_ref, hl_ref, ext_ref, h_ref, *, c, t_valid, width):
    ci = pl.program_id(1)
    npairs = width // LANES

    @pl.when(ci == 0)
    def _():
        ext_ref[0:SUBLANES, :] = conv0_ref[0]
        h_ref[0:1, :] = h0_ref[0]

    px = px_ref[0]
    ext_ref[SUBLANES:SUBLANES + c, :] = px
    acc = None
    for j in range(CONV_W - 1):
        term = cw_ref[j:j + 1, :] * ext_ref[SUBLANES - (CONV_W - 1) + j:SUBLANES - (CONV_W - 1) + j + c, :]
        acc = term if acc is None else acc + term
    xc = cb_ref[...] + (acc + cw_ref[CONV_W - 1:CONV_W, :] * px)
    ext_ref[0:SUBLANES, :] = ext_ref[c:c + SUBLANES, :]

    xcb = xc.astype(BF16)
    ga, gx = [], []
    for pi in range(npairs):
        sl = slice(pi * LANES, (pi + 1) * LANES)
        ga.append(_dot(xcb[:, sl], wa_ref[pi]))
        gx.append(_dot(xcb[:, sl], wx_ref[pi]))
    gate_r = jax.nn.sigmoid(jnp.concatenate(ga, axis=1) + ba_ref[...])
    gate_i = jax.nn.sigmoid(jnp.concatenate(gx, axis=1) + bx_ref[...])
    log_a = (-LRU_C * gate_r) * _softplus(-lam_ref[...])
    a_t = jnp.exp(log_a)
    b_t = jnp.sqrt(-jnp.tanh(log_a) * (a_t * a_t + 1.0)) * (gate_i * xc)

    row = lax.broadcasted_iota(jnp.int32, (c, 1), 0)
    b_t = b_t + jnp.where(row == 0, a_t * h_ref[0:1, :], 0.0)
    d = 1
    while d < c:
        keep = row >= d
        a_sh = jnp.where(keep, pltpu.roll(a_t, d, axis=0), 1.0)
        b_sh = jnp.where(keep, pltpu.roll(b_t, d, axis=0), 0.0)
        b_t = a_t * b_sh + b_t
        a_t = a_t * a_sh
        d *= 2
    hs = b_t
    h_ref[0:1, :] = hs[c - 1:c, :]
    o_ref[0] = hs * _gelu_tanh(pg_ref[0])

    last = t_valid - 1

    @pl.when(ci == last // c)
    def _():
        hl_ref[0] = hs[last % c:last % c + 1, :]


def _lru_mix(p_x, p_gate, conv0_pad, h0, prm, c, t_valid):
    b, tp, width = p_x.shape
    npairs = width // LANES
    assert tp % c == 0 and c >= SUBLANES
    vec = lambda n: pl.BlockSpec((1, n), lambda i, j: (0, 0))
    blk = pl.BlockSpec((npairs, LANES, LANES), lambda i, j: (0, 0, 0))
    return pl.pallas_call(
        functools.partial(_lru_kernel, c=c, t_valid=t_valid, width=width),
        grid=(b, tp // c),
        in_specs=[pl.BlockSpec((1, c, width), lambda i, j: (i, j, 0)),
                  pl.BlockSpec((1, c, width), lambda i, j: (i, j, 0)),
                  pl.BlockSpec((1, SUBLANES, width), lambda i, j: (i, 0, 0)),
                  pl.BlockSpec((1, 1, width), lambda i, j: (i, 0, 0)),
                  pl.BlockSpec((CONV_W, width), lambda i, j: (0, 0)),
                  vec(width), blk, vec(width), blk, vec(width), vec(width)],
        out_specs=[pl.BlockSpec((1, c, width), lambda i, j: (i, j, 0)),
                   pl.BlockSpec((1, 1, width), lambda i, j: (i, 0, 0))],
        out_shape=[jax.ShapeDtypeStruct((b, tp, width), F32),
                   jax.ShapeDtypeStruct((b, 1, width), F32)],
        scratch_shapes=[pltpu.VMEM((c + SUBLANES, width), F32),
                        pltpu.VMEM((SUBLANES, width), F32)],
        compiler_params=_cparams(("parallel", "arbitrary"), 32),
        name="lru_mix",
    )(p_x, p_gate, conv0_pad, h0, prm["cw"], prm["cb"], prm["wa"], prm["ba"], prm["wx"], prm["bx"],
      prm["lam"])


def _sb_weights(zs, cums, upper2, masks):
    sps = [jnp.maximum(z, 0.0) + jnp.log(1.0 + jnp.exp(-jnp.abs(z))) for z in zs]
    sps = [sp if m is None else jnp.where(m, sp, 0.0) for sp, m in zip(sps, masks)]
    suffix = [_dot(jnp.concatenate(_split2(sp), axis=1), upper2) for sp in sps]
    atts = [jnp.exp((z - c) - sf) for z, c, sf in zip(zs, cums, suffix)]
    atts = [a if m is None else jnp.where(m, a, 0.0) for a, m in zip(atts, masks)]
    cums = [c + jnp.sum(sp, axis=-1, keepdims=True) for c, sp in zip(cums, sps)]
    return atts, cums


def _upper2(n):
    jj = lax.broadcasted_iota(jnp.int32, (2 * n, n), 0) & (n - 1)
    ss = lax.broadcasted_iota(jnp.int32, (2 * n, n), 1)
    return jnp.where(jj >= ss, 1.0, 0.0).astype(BF16)


def _sb_prompt_kernel(bias_ref, q_ref, k_ref, v_ref, o_ref, z_ref, att_ref, acc_ref, cum_ref):
    pi = pl.program_id(1)
    qi = pl.program_id(2)
    blk = SB_TILE
    lane = lax.broadcasted_iota(jnp.int32, (1, LANES), 1)
    first = lane < SB_HD
    q = q_ref[0] * (SB_HD ** -0.5)
    qs = [jnp.where(first, q, 0.0).astype(BF16), jnp.where(first, 0.0, q).astype(BF16)]
    biases = [bias_ref[2 * pi], bias_ref[2 * pi + 1]]
    upper2 = _upper2(blk)

    def scores(kb):
        start = pl.multiple_of(kb * blk, blk)
        kblk = k_ref[0, pl.ds(start, blk), :].astype(BF16)
        return [_dot_nt(qh, kblk) + bh for qh, bh in zip(qs, biases)]

    def weighted_values(atts, kb):
        start = pl.multiple_of(kb * blk, blk)
        vblk = v_ref[0, pl.ds(start, blk), :].astype(BF16)
        return [_dot(a, vblk) for a in atts]

    qpos = lax.broadcasted_iota(jnp.int32, (blk, blk), 0)
    kpos = lax.broadcasted_iota(jnp.int32, (blk, blk), 1)
    mask = kpos < qpos
    zero = jnp.zeros((blk, 1), F32)
    heads2 = range(2)
    atts, cums = _sb_weights(scores(qi), [zero, zero], upper2, [mask, mask])
    zs = scores(jnp.maximum(qi - 1, 0))
    for h in heads2:
        att_ref[0, h] = atts[h].astype(BF16)
        z_ref[0, h] = zs[h]
        acc_ref[h] = jnp.zeros((blk, LANES), F32)
        cum_ref[h] = cums[h]

    def trip(kb, cur, nxt):
        zs_next = scores(jnp.maximum(kb - 1, 0))
        for h in heads2:
            z_ref[nxt, h] = zs_next[h]
        pvs = weighted_values([att_ref[cur, h] for h in heads2], kb + 1)
        for h in heads2:
            acc_ref[h] += pvs[h]
        atts, cums = _sb_weights([z_ref[cur, h] for h in heads2], [cum_ref[h] for h in heads2],
                                 upper2, [None, None])
        for h in heads2:
            att_ref[nxt, h] = atts[h].astype(BF16)
            cum_ref[h] = cums[h]

    def body(j, carry):
        kb = qi - 1 - 2 * j
        trip(kb, 0, 1)

        @pl.when(kb >= 1)
        def _():
            trip(kb - 1, 1, 0)

        return carry

    lax.fori_loop(0, (qi + 1) // 2, body, 0)
    last = qi & 1
    pvs = weighted_values([att_ref[last, h] for h in heads2], 0)
    o_ref[0] = jnp.where(first, acc_ref[0] + pvs[0], acc_ref[1] + pvs[1])


def _sb_prompt(qkv, bias, heads):
    b, t, w3 = qkv.shape
    w = w3 // 3
    npairs = w // LANES
    assert t % SB_TILE == 0 and heads * SB_HD == w
    return pl.pallas_call(
        _sb_prompt_kernel,
        grid=(b, npairs, t // SB_TILE),
        in_specs=[pl.BlockSpec(memory_space=pltpu.SMEM),
                  pl.BlockSpec((1, SB_TILE, LANES), lambda i, p, j: (i, j, p)),
                  pl.BlockSpec((1, t, LANES), lambda i, p, j: (i, 0, npairs + p)),
                  pl.BlockSpec((1, t, LANES), lambda i, p, j: (i, 0, 2 * npairs + p))],
        out_specs=pl.BlockSpec((1, SB_TILE, LANES), lambda i, p, j: (i, j, p)),
        out_shape=jax.ShapeDtypeStruct((b, t, w), F32),
        scratch_shapes=[pltpu.VMEM((2, 2, SB_TILE, SB_TILE), F32),
                        pltpu.VMEM((2, 2, SB_TILE, SB_TILE), BF16),
                        pltpu.VMEM((2, SB_TILE, LANES), F32),
                        pltpu.VMEM((2, SB_TILE, 1), F32)],
        compiler_params=_cparams(("parallel", "parallel", "arbitrary"), 32),
        name="sb_prompt",
    )(bias, qkv, qkv, qkv)


def _sb_decode_kernel(pt_ref, bias_ref, q_ref, *refs, pages_per_step, heads):
    k_refs = refs[:pages_per_step]
    v_refs = refs[pages_per_step:2 * pages_per_step]
    o_ref, acc_ref, cum_ref = refs[2 * pages_per_step:]
    j = pl.program_id(1)
    w = heads * SB_HD

    @pl.when(j == 0)
    def _():
        acc_ref[...] = jnp.zeros_like(acc_ref)
        cum_ref[...] = jnp.zeros_like(cum_ref)

    hrow = lax.broadcasted_iota(jnp.int32, (heads, w), 0)
    hcol = lax.broadcasted_iota(jnp.int32, (heads, w), 1) >> (SB_HD.bit_length() - 1)
    own = hrow == hcol
    qm = jnp.where(own, q_ref[0] * (SB_HD ** -0.5), 0.0).astype(BF16)
    upper2 = _upper2(PAGE)
    bias = bias_ref[...]

    pages = range(pages_per_step)
    zs = [_dot(qm, k_refs[i][...].astype(BF16)) + bias for i in pages]
    none = [None] * pages_per_step
    zero = jnp.zeros((heads, 1), F32)
    atts, tots = _sb_weights(zs, [zero] * pages_per_step, upper2, none)
    acc = acc_ref[...]
    cum = cum_ref[...]
    for i in pages:
        att = atts[i] * jnp.exp(-cum)
        acc = acc + _dot_nt(att.astype(BF16), v_refs[i][...].astype(BF16))
        cum = cum + tots[i]
    acc_ref[...] = acc
    cum_ref[...] = cum

    @pl.when(j == pl.num_programs(1) - 1)
    def _():
        o_ref[0] = jnp.sum(jnp.where(own, acc, 0.0), axis=0, keepdims=True)


def _sb_decode(q, k_t, v_t, page_table, bias, heads, pages_per_step):
    b, _, w = q.shape
    n_pages = page_table.shape[1]
    assert n_pages % pages_per_step == 0 and w == heads * SB_HD and k_t.shape[1] == PAGE
    steps = n_pages // pages_per_step

    def page_spec(i):
        return pl.BlockSpec((w, PAGE),
                            lambda bi, j, pt: (pt[bi, n_pages - 1 - (j * pages_per_step + i)], 0))

    return pl.pallas_call(
        functools.partial(_sb_decode_kernel, pages_per_step=pages_per_step, heads=heads),
        grid_spec=pltpu.PrefetchScalarGridSpec(
            num_scalar_prefetch=1,
            grid=(b, steps),
            in_specs=[pl.BlockSpec((heads, 1), lambda bi, j, pt: (0, 0)),
                      pl.BlockSpec((1, 1, w), lambda bi, j, pt: (bi, 0, 0))]
                     + [page_spec(i) for i in range(pages_per_step)] * 2,
            out_specs=pl.BlockSpec((1, 1, w), lambda bi, j, pt: (bi, 0, 0)),
            scratch_shapes=[pltpu.VMEM((heads, w), F32), pltpu.VMEM((heads, 1), F32)]),
        out_shape=jax.ShapeDtypeStruct((b, 1, w), F32),
        compiler_params=_cparams(("parallel", "arbitrary"), 48),
        name="sb_decode",
    )(page_table, bias.reshape(heads, 1), q, *([k_t] * pages_per_step), *([v_t] * pages_per_step))


def _pair_blockdiag(w):
    n, d, _ = w.shape
    w = w.reshape(n // 2, 2, d, d)
    z = jnp.zeros_like(w[:, 0])
    top = jnp.concatenate([w[:, 0], z], axis=2)
    bot = jnp.concatenate([z, w[:, 1]], axis=2)
    return jnp.concatenate([top, bot], axis=1)


def _state_to_pairs(s):
    b, h, n, _ = s.shape
    return _pair_blockdiag(s.reshape(b * h, n, n)).reshape(b, h // 2, 2 * n, 2 * n)


def _pairs_to_state(sp, n):
    b, hp = sp.shape[0], sp.shape[1]
    return jnp.stack([sp[:, :, :n, :n], sp[:, :, n:, n:]], axis=2).reshape(b, 2 * hp, n, n)


def _pad_rows(x, multiple):
    t = x.shape[1]
    t_pad = -(-t // multiple) * multiple
    return x if t_pad == t else jnp.pad(x, ((0, 0), (0, t_pad - t), (0, 0)))


def kernel(x_prompt, x_sample, state_rwkv_wkv, state_rwkv_shift, state_lru_h, state_lru_conv,
           cache_sb_k, cache_sb_v, page_table, cache_mem_k, cache_mem_v, mem_prompt,
           g_mix, g_mem, g_memkv, g_ffn, wq_mem, wk_mem, wv_mem, wo_mem, qn_mem, kn_mem,
           w_ffn_gate, w_ffn_up, w_ffn_down, w_in_ab, mu_shift, rw_w0, rw_w2, rw_a0, rw_a2,
           rw_g2, rw_kk, rw_ka, rw_rk, rw_lnx_w, rw_lnx_b, lru_conv_w, lru_conv_b, lru_wa,
           lru_ba, lru_wx, lru_bx, lru_lambda, w_out_ab, w_qkv_sb, w_out_sb, sb_bias):
    depth, d_model = g_mix.shape
    mem_heads = cache_mem_k.shape[3]
    mem_w = wq_mem.shape[2]
    rw_w = rw_w0.shape[1]
    rw_cols = mu_shift.shape[1]
    lru_w = lru_lambda.shape[1]
    rw_heads = rw_w // RW_HD
    sb_heads = sb_bias.shape[1]
    sb_w = sb_heads * SB_HD
    d_ff = w_ffn_gate.shape[2]
    lora_w = rw_w2.shape[1]
    lora_a = rw_a2.shape[1]
    assert lora_w + lora_a == LANES and rw_g2.shape[1] == LANES
    tf = 256
    assert d_ff % tf == 0
    bf = lambda z: z.astype(BF16)

    wg3 = [bf(w_ffn_gate[i]).reshape(d_model, d_ff // tf, tf).transpose(1, 0, 2) for i in range(depth)]
    wu3 = [bf(w_ffn_up[i]).reshape(d_model, d_ff // tf, tf).transpose(1, 0, 2) for i in range(depth)]
    wd3 = [bf(w_ffn_down[i]).reshape(d_ff // tf, tf, d_model) for i in range(depth)]
    head_id = jnp.arange(LANES) // RW_HD
    bd = (head_id[:, None] == head_id[None, :]).astype(BF16)

    def even_params(e):
        zw = jnp.zeros((lora_a, rw_w), F32)
        za = jnp.zeros((lora_w, rw_w), F32)
        row = lambda z: z.reshape(1, -1)
        rw = dict(mu=row(mu_shift[e]), w0=row(rw_w0[e]), w2=bf(jnp.concatenate([rw_w2[e], zw], axis=0)),
                  a0=row(rw_a0[e]), a2=bf(jnp.concatenate([za, rw_a2[e]], axis=0)), g2=bf(rw_g2[e]),
                  kk=row(rw_kk[e]), ka=row(rw_ka[e]), rk=row(rw_rk[e]), lnw=row(rw_lnx_w[e]),
                  lnb=row(rw_lnx_b[e]), bd=bd)
        lru = dict(cw=lru_conv_w[e], cb=row(lru_conv_b[e]), wa=bf(_pair_blockdiag(lru_wa[e])),
                   ba=row(lru_ba[e]), wx=bf(_pair_blockdiag(lru_wx[e])), bx=row(lru_bx[e]),
                   lam=row(lru_lambda[e]))
        return rw, lru

    def trunk(x, sample):
        bsz, t, _ = x.shape
        m = bsz * t
        tm = 512 if m % 512 == 0 else m
        out = {}
        xf = x.reshape(m, d_model)
        for i in range(depth):
            if i % 2 == 0:
                e = i // 2
                rw, lru = even_params(e)
                p_rw, p_x, p_gate = _norm_proj(xf, g_mix[i], bf(w_in_ab[e]),
                                               [rw_cols, rw_cols + lru_w], tm)
                p_rw = p_rw.reshape(bsz, t, rw_cols)
                p_x = p_x.reshape(bsz, t, lru_w)
                p_gate = p_gate.reshape(bsz, t, lru_w)
                if sample:
                    wkv0, shift0 = state_rwkv_wkv[e], state_rwkv_shift[e]
                    h0, conv0 = state_lru_h[e], state_lru_conv[e]
                else:
                    wkv0 = jnp.zeros((bsz, rw_heads, RW_HD, RW_HD), F32)
                    shift0 = jnp.zeros((bsz, rw_cols), F32)
                    h0 = jnp.zeros((bsz, lru_w), F32)
                    conv0 = jnp.zeros((bsz, CONV_W - 1, lru_w), F32)
                o_rw, s_pairs = _rwkv_mix(_pad_rows(p_rw, RW_CHUNK), shift0.reshape(bsz, 1, rw_cols),
                                          _state_to_pairs(wkv0), rw, RW_CHUNK, t)
                conv0_pad = jnp.pad(conv0, ((0, 0), (SUBLANES - (CONV_W - 1), 0), (0, 0)))
                px_pad = _pad_rows(p_x, BF16_ROWS)
                c_lru = LRU_CHUNK if px_pad.shape[1] % LRU_CHUNK == 0 else px_pad.shape[1]
                o_lru, h_last = _lru_mix(px_pad, _pad_rows(p_gate, BF16_ROWS), conv0_pad,
                                         h0.reshape(bsz, 1, lru_w), lru, c_lru, t)
                out.setdefault("wkv", []).append(_pairs_to_state(s_pairs, RW_HD))
                out.setdefault("shift", []).append(p_rw[:, t - 1])
                out.setdefault("lru_h", []).append(h_last[:, 0])
                out.setdefault("lru_conv", []).append(
                    jnp.concatenate([conv0, p_x], axis=1)[:, -(CONV_W - 1):])
                w_out = bf(w_out_ab[e])
                xf = _proj_residual(xf, [o_rw[:, :t].reshape(m, rw_w), o_lru[:, :t].reshape(m, lru_w)],
                                    [w_out[:rw_w], w_out[rw_w:]], tm)
            else:
                o = i // 2
                (qkv,) = _norm_proj(xf, g_mix[i], bf(w_qkv_sb[o]), [], tm)
                qkv = qkv.reshape(bsz, t, 3 * sb_w)
                if sample:
                    pool = cache_sb_k.shape[1]
                    k_t = jnp.transpose(cache_sb_k, (0, 1, 3, 4, 2)).reshape(-1, PAGE)
                    v_t = jnp.transpose(cache_sb_v, (0, 1, 3, 4, 2)).reshape(-1, PAGE)
                    att = _sb_decode(qkv[:, :, :sb_w], k_t, v_t, page_table + o * pool, sb_bias[o],
                                     sb_heads, 4)
                else:
                    att = _sb_prompt(qkv, sb_bias[o], sb_heads)
                out.setdefault("sb_k", []).append(qkv[:, :, sb_w:2 * sb_w].reshape(bsz, t, sb_heads, SB_HD))
                out.setdefault("sb_v", []).append(qkv[:, :, 2 * sb_w:].reshape(bsz, t, sb_heads, SB_HD))
                xf = _proj_residual(xf, [att.reshape(m, sb_w)], [bf(w_out_sb[o])], tm)
            if sample:
                mk = cache_mem_k[i].reshape(bsz, -1, mem_w)
                mv = cache_mem_v[i].reshape(bsz, -1, mem_w)
            else:
                mk, mv = _memory_kv(mem_prompt, g_memkv[i], bf(wk_mem[i]), bf(wv_mem[i]), kn_mem[i],
                                    mem_heads)
                out.setdefault("mem_k", []).append(mk.reshape(bsz, -1, mem_heads, mem_w // mem_heads))
                out.setdefault("mem_v", []).append(mv.reshape(bsz, -1, mem_heads, mem_w // mem_heads))
            x3 = _pad_rows(xf.reshape(bsz, t, d_model), BF16_ROWS)
            x3 = _memx_residual(x3, g_mem[i], bf(wq_mem[i]), qn_mem[i], mk, mv, bf(wo_mem[i]),
                                mem_heads, 512 if x3.shape[1] % 512 == 0 else x3.shape[1])
            xf = x3[:, :t].reshape(m, d_model)
            xf = _ffn_residual(xf, g_ffn[i], wg3[i], wu3[i], wd3[i], tm)
        return xf.reshape(bsz, t, d_model), out

    y_p, sp = trunk(x_prompt, False)
    y_s, ss = trunk(x_sample, True)
    st = lambda arrs: jnp.stack(arrs, axis=0)
    return (y_p, y_s, st(sp["wkv"]), st(ss["wkv"]), st(sp["shift"]), st(ss["shift"]),
            st(sp["lru_h"]), st(ss["lru_h"]), st(sp["lru_conv"]), st(ss["lru_conv"]),
            st(sp["sb_k"]), st(ss["sb_k"]), st(sp["sb_v"]), st(ss["sb_v"]),
            st(sp["mem_k"]), st(sp["mem_v"]))
```

```python
import functools
import math

import jax
import jax.numpy as jnp
from jax import lax
from jax.experimental import pallas as pl
from jax.experimental.pallas import tpu as pltpu

F32 = jnp.float32
BF16 = jnp.bfloat16

NORM_EPS = 1e-6
RW_HD = 64
LANES = 128
SUBLANES = 8
BF16_ROWS = 16
RW_CHUNK = 64
RW_ROWS_PER_STEP = 4
MEMX_ROWS_PER_STEP = 8
DECODE_PAGES_PER_STEP = 8
LRU_CHUNK = 256
RW_LN_EPS = RW_HD * 1e-5
LRU_C = 8.0
LRU_BD = 64
CONV_W = 4
SB_HD = 64
PAGE = 128
SB_TILE = 256
EXP_M05 = math.exp(-0.5)
VMEM_BYTES_V7X = 64 * 1024 * 1024


def _cparams(sem, vmem_mb):
    assert vmem_mb * 1024 * 1024 < VMEM_BYTES_V7X
    return pltpu.CompilerParams(dimension_semantics=sem, vmem_limit_bytes=vmem_mb * 1024 * 1024)


def _dot(a, b):
    return jnp.dot(a, b, preferred_element_type=F32)


def _dot_nt(a, b):
    return lax.dot_general(a, b, (((1,), (1,)), ((), ())), preferred_element_type=F32)


def _dot_tn(a, b):
    return lax.dot_general(a, b, (((0,), (0,)), ((), ())), preferred_element_type=F32)


def _split2(x):
    hi = x.astype(BF16)
    lo = (x - hi.astype(F32)).astype(BF16)
    return hi, lo


def _split3(x):
    hi = x.astype(BF16)
    r = x - hi.astype(F32)
    mid = r.astype(BF16)
    lo = (r - mid.astype(F32)).astype(BF16)
    return hi, mid, lo


def _dot_xw(x, w_bf16, parts=2):
    pieces = _split2(x) if parts == 2 else _split3(x)
    out = _dot(pieces[0], w_bf16)
    for p in pieces[1:]:
        out = out + _dot(p, w_bf16)
    return out


def _dot_wx(w_bf16, x, parts=3):
    pieces = _split2(x) if parts == 2 else _split3(x)
    out = _dot(w_bf16, pieces[0])
    for p in pieces[1:]:
        out = out + _dot(w_bf16, p)
    return out


def _mm3(a, b, dot):
    ah, al = _split2(a)
    bh, bl = _split2(b)
    return dot(ah, bh) + (dot(al, bh) + dot(ah, bl))


def _mm1(a, b, dot):
    return dot(a.astype(BF16), b.astype(BF16))


def _softplus(x):
    return jnp.maximum(x, 0.0) + jnp.log1p(jnp.exp(-jnp.abs(x)))


def _gelu_tanh(x):
    return 0.5 * x * (1.0 + jnp.tanh(math.sqrt(2.0 / math.pi) * (x + 0.044715 * (x * x * x))))


def _rms(x, g, eps=NORM_EPS):
    ms = jnp.mean(x * x, axis=-1, keepdims=True)
    return x * lax.rsqrt(ms + eps) * g


def _norm_proj_kernel(x_ref, g_ref, w_ref, *o_refs, splits, chunk):
    h = _rms(x_ref[...], g_ref[...]).astype(BF16)
    for o_ref, (s, e) in zip(o_refs, splits):
        for c0 in range(s, e, chunk):
            c1 = min(c0 + chunk, e)
            o_ref[:, c0 - s:c1 - s] = _dot(h, w_ref[:, c0:c1])


def _norm_proj(x, g, w_bf16, splits, tm):
    m, d = x.shape
    n = w_bf16.shape[1]
    assert m % tm == 0
    bounds = [0] + list(splits) + [n]
    ranges = [(bounds[i], bounds[i + 1]) for i in range(len(bounds) - 1)]
    out_shape = [jax.ShapeDtypeStruct((m, e - s), F32) for s, e in ranges]
    out_specs = [pl.BlockSpec((tm, e - s), lambda i: (i, 0)) for s, e in ranges]
    return pl.pallas_call(
        functools.partial(_norm_proj_kernel, splits=ranges, chunk=512),
        grid=(m // tm,),
        in_specs=[pl.BlockSpec((tm, d), lambda i: (i, 0)),
                  pl.BlockSpec((1, d), lambda i: (0, 0)),
                  pl.BlockSpec((d, n), lambda i: (0, 0))],
        out_specs=out_specs,
        out_shape=out_shape,
        compiler_params=_cparams(("parallel",), 48),
        name="norm_proj",
    )(x, g.reshape(1, d), w_bf16)


def _ffn_kernel(x_ref, g_ref, wg_ref, wu_ref, wd_ref, o_ref, h_ref):
    x = x_ref[...]
    h_ref[...] = _rms(x, g_ref[...]).astype(BF16)
    o_ref[...] = x

    def body(j, carry):
        h = h_ref[...]
        gate = _dot(h, wg_ref[j])
        up = _dot(h, wu_ref[j])
        act = (gate * jax.nn.sigmoid(gate) * up).astype(BF16)
        o_ref[...] += _dot(act, wd_ref[j])
        return carry

    lax.fori_loop(0, wg_ref.shape[0], body, 0)


def _ffn_residual(x, g, wg3, wu3, wd3, tm):
    m, d = x.shape
    nf, _, tf = wg3.shape
    assert m % tm == 0
    return pl.pallas_call(
        _ffn_kernel,
        grid=(m // tm,),
        in_specs=[pl.BlockSpec((tm, d), lambda i: (i, 0)),
                  pl.BlockSpec((1, d), lambda i: (0, 0)),
                  pl.BlockSpec((nf, d, tf), lambda i: (0, 0, 0)),
                  pl.BlockSpec((nf, d, tf), lambda i: (0, 0, 0)),
                  pl.BlockSpec((nf, tf, d), lambda i: (0, 0, 0))],
        out_specs=pl.BlockSpec((tm, d), lambda i: (i, 0)),
        out_shape=jax.ShapeDtypeStruct((m, d), F32),
        scratch_shapes=[pltpu.VMEM((tm, d), BF16)],
        compiler_params=_cparams(("parallel",), 56),
        name="ffn_residual",
    )(x, g.reshape(1, d), wg3, wu3, wd3)


def _memkv_kernel(m_ref, g_ref, wk_ref, wv_ref, kn_ref, k_ref, v_ref, *, heads, hd):
    mn = _rms(m_ref[0], g_ref[...]).astype(BF16)
    k = _dot(mn, wk_ref[...])
    v_ref[0] = _dot(mn, wv_ref[...])
    for hh in range(heads):
        sl = slice(hh * hd, (hh + 1) * hd)
        k_ref[0, :, sl] = _rms(k[:, sl], kn_ref[...])


def _memory_kv(mem, g, wk, wv, kn, heads):
    b, nm, d = mem.shape
    w = wk.shape[1]
    hd = w // heads
    return pl.pallas_call(
        functools.partial(_memkv_kernel, heads=heads, hd=hd),
        grid=(b,),
        in_specs=[pl.BlockSpec((1, nm, d), lambda i: (i, 0, 0)),
                  pl.BlockSpec((1, d), lambda i: (0, 0)),
                  pl.BlockSpec((d, w), lambda i: (0, 0)),
                  pl.BlockSpec((d, w), lambda i: (0, 0)),
                  pl.BlockSpec((1, hd), lambda i: (0, 0))],
        out_specs=[pl.BlockSpec((1, nm, w), lambda i: (i, 0, 0)),
                   pl.BlockSpec((1, nm, w), lambda i: (i, 0, 0))],
        out_shape=[jax.ShapeDtypeStruct((b, nm, w), F32)] * 2,
        compiler_params=_cparams(("parallel",), 32),
        name="memory_kv",
    )(mem, g.reshape(1, d), wk, wv, kn.reshape(1, hd))


def _memx_kernel(*refs, n_proj, heads, hd):
    x_ref, a_refs, w_refs = refs[0], refs[1:1 + n_proj], refs[1 + n_proj:1 + 2 * n_proj]
    g_ref, wq_ref, qn_ref, k_ref, v_ref, wo_ref, o_ref = refs[1 + 2 * n_proj:]
    nb = x_ref.shape[0]
    scale = hd ** -0.5
    x1 = []
    for bi in range(nb):
        x = x_ref[bi]
        for a_ref, w_ref in zip(a_refs, w_refs):
            x = x + _dot(a_ref[bi].astype(BF16), w_ref[...])
        x1.append(x)
    qs = [_dot(_rms(x, g_ref[...]).astype(BF16), wq_ref[...]) for x in x1]
    chains = [(bi, slice(hh * hd, (hh + 1) * hd)) for bi in range(nb) for hh in range(heads)]
    qh = [_rms(qs[bi][:, sl], qn_ref[...]).astype(BF16) for bi, sl in chains]
    s = [_dot_nt(qh[c], k_ref[bi, :, sl].astype(BF16)) * scale for c, (bi, sl) in enumerate(chains)]
    e = [jnp.exp(z - jnp.max(z, axis=-1, keepdims=True)) for z in s]
    p = [(z / jnp.sum(z, axis=-1, keepdims=True)).astype(BF16) for z in e]
    oh = [_dot(p[c], v_ref[bi, :, sl].astype(BF16)) for c, (bi, sl) in enumerate(chains)]
    for bi in range(nb):
        o = jnp.concatenate(oh[bi * heads:(bi + 1) * heads], axis=1).astype(BF16)
        o_ref[bi] = x1[bi] + _dot(o, wo_ref[...])


def _memx_residual(x, a_list, w_list, g, wq, qn, k, v, wo, heads, tm, nb):
    b, t, d = x.shape
    nm, w = k.shape[1], k.shape[2]
    hd = w // heads
    assert t % tm == 0 and b % nb == 0
    rows = lambda n: pl.BlockSpec((nb, tm, n), lambda i, j: (i, j, 0))
    full = lambda arr: pl.BlockSpec(arr.shape, lambda i, j: (0,) * arr.ndim)
    mem = pl.BlockSpec((nb, nm, w), lambda i, j: (i, 0, 0))
    g2, qn2 = g.reshape(1, d), qn.reshape(1, hd)
    return pl.pallas_call(
        functools.partial(_memx_kernel, n_proj=len(a_list), heads=heads, hd=hd),
        grid=(b // nb, t // tm),
        in_specs=[rows(d)] + [rows(a.shape[2]) for a in a_list] + [full(wi) for wi in w_list]
                 + [full(g2), full(wq), full(qn2), mem, mem, full(wo)],
        out_specs=rows(d),
        out_shape=jax.ShapeDtypeStruct((b, t, d), F32),
        compiler_params=_cparams(("parallel", "parallel"), 40),
        name="memx_residual",
    )(x, *a_list, *w_list, g2, wq, qn2, k, v, wo)


def _unit_lower_inverse(lmats, n, block):
    row = lax.broadcasted_iota(jnp.int32, (n, n), 0)
    col = lax.broadcasted_iota(jnp.int32, (n, n), 1)
    eye = jnp.where(row == col, 1.0, 0.0).astype(F32)
    first = ((row ^ col) < 2) & ((row & 1) != 0) & ((col & 1) == 0)
    ts = [eye - jnp.where(first, l, 0.0) for l in lmats]
    s = 2
    while s < block:
        lower_left = ((row ^ col) < 2 * s) & ((row & s) != 0) & ((col & s) == 0)
        us = [_mm1(jnp.where(lower_left, l, 0.0), t, _dot) for l, t in zip(lmats, ts)]
        ts = [t - _mm1(t, u, _dot) for t, u in zip(ts, us)]
        s *= 2
    return ts


def _rwkv_kernel(p_ref, sh0_ref, s0_ref, mu_ref, w0_ref, w2_ref, a0_ref, a2_ref, g2_ref,
                 kk_ref, ka_ref, rk_ref, lw_ref, lb_ref, bd_ref,
                 o_ref, s_out_ref, s_ref, prev_ref, *, c, t_valid, width):
    ci = pl.program_id(1)
    nb = p_ref.shape[0]
    npairs = width // LANES
    sls = [slice(pi * LANES, (pi + 1) * LANES) for pi in range(npairs)]

    @pl.when(ci == 0)
    def _():
        s_ref[...] = s0_ref[...]
        prev_ref[:, 0:1, :] = sh0_ref[...]

    bd = bd_ref[...]

    def head_sum(z):
        rows = jnp.concatenate([z[:, sl] for sl in sls], axis=0)
        tot = _dot_xw(rows, bd)
        return jnp.concatenate([tot[i * c:(i + 1) * c] for i in range(npairs)], axis=1)

    row1 = lax.broadcasted_iota(jnp.int32, (c, 1), 0)
    rr = lax.broadcasted_iota(jnp.int32, (c, c), 0)
    cc = lax.broadcasted_iota(jnp.int32, (c, c), 1)
    tril = jnp.where(cc <= rr, 1.0, 0.0).astype(BF16)

    def prepare(bi):
        p = p_ref[bi]
        prev = jnp.where(row1 == 0, prev_ref[bi, 0:1, :], pltpu.roll(p, 1, axis=0))
        prev_ref[bi, 0:1, :] = p[c - 1:c, :]
        xs = p + mu_ref[...] * (prev - p)
        r = xs[:, 0:width]
        k = xs[:, width:2 * width]
        v = xs[:, 2 * width:3 * width]
        dwa = xs[:, 3 * width:3 * width + LANES]
        dg = xs[:, 3 * width + LANES:3 * width + 2 * LANES]
        u = w0_ref[...] + _dot(jnp.tanh(dwa).astype(BF16), w2_ref[...])
        logw = -EXP_M05 * jax.nn.sigmoid(u)
        a = jax.nn.sigmoid(a0_ref[...] + _dot(dwa.astype(BF16), a2_ref[...]))
        g = _dot(jax.nn.sigmoid(dg).astype(BF16), g2_ref[...])
        kkr = k * kk_ref[...]
        kk = kkr / jnp.maximum(jnp.sqrt(head_sum(kkr * kkr)), 1e-12)
        kmod = k * (1.0 + (a - 1.0) * ka_ref[...])
        if t_valid % c != 0:
            valid = ((ci * c + row1) < t_valid).astype(F32)
            logw, kk, kmod, v = logw * valid, kk * valid, kmod * valid, v * valid
        b = kk * a
        lc = _dot_wx(tril, logw, parts=3)
        lc_end = lc[c - 1:c, :]
        inv_c = jnp.exp(-lc)
        to_end = jnp.exp(lc_end - lc)
        return dict(r=r, v=v, kmod=kmod, g=g, kk_t=kk * jnp.exp(lc - logw), b_t=b * inv_c,
                    k_t=kmod * inv_c, r_t=r * jnp.exp(lc), b_end=b * to_end, k_end=kmod * to_end,
                    c_end=jnp.exp(lc_end))

    preps = [prepare(bi) for bi in range(nb)]

    n2 = 2 * c
    lane = lax.broadcasted_iota(jnp.int32, (1, LANES), 1)
    m0 = (lane < RW_HD).astype(F32)
    m1 = 1.0 - m0
    r2 = lax.broadcasted_iota(jnp.int32, (n2, n2), 0)
    c2 = lax.broadcasted_iota(jnp.int32, (n2, n2), 1)
    strict = c2 < r2
    incl = c2 <= r2

    def stack(z):
        return jnp.concatenate([z * m0, z * m1], axis=0).astype(BF16)

    chains = [(bi, pi) for bi in range(nb) for pi in range(npairs)]
    ids = range(len(chains))
    stk = lambda name: [stack(preps[bi][name][:, sls[pi]]) for bi, pi in chains]
    kks, bs, ks, rs, vs = stk("kk_t"), stk("b_t"), stk("k_t"), stk("r_t"), stk("v")
    s_old = [s_ref[bi, pi] for bi, pi in chains]
    s_bf = [s.astype(BF16) for s in s_old]
    a_kb = [jnp.where(strict, _dot_nt(kks[i], bs[i]), 0.0) for i in ids]
    a_kk = [jnp.where(strict, _dot_nt(kks[i], ks[i]), 0.0).astype(BF16) for i in ids]
    rhs = [-(_dot_nt(kks[i], s_bf[i]) + _dot(a_kk[i], vs[i])) for i in ids]
    tinv = _unit_lower_inverse(a_kb, n2, c)
    sas = [_mm1(tinv[i], rhs[i], _dot).astype(BF16) for i in ids]
    a_rb = [jnp.where(incl, _dot_nt(rs[i], bs[i]), 0.0).astype(BF16) for i in ids]
    a_rk = [jnp.where(incl, _dot_nt(rs[i], ks[i]), 0.0).astype(BF16) for i in ids]
    y2 = [_dot_nt(rs[i], s_bf[i]) + _dot(a_rb[i], sas[i]) + _dot(a_rk[i], vs[i]) for i in ids]
    for i, (bi, pi) in enumerate(chains):
        s_ref[bi, pi] = (s_old[i] * preps[bi]["c_end"][:, sls[pi]]
                         + _dot_tn(sas[i], stack(preps[bi]["b_end"][:, sls[pi]]))
                         + _dot_tn(vs[i], stack(preps[bi]["k_end"][:, sls[pi]])))

    inv_n = 1.0 / RW_HD
    for bi in range(nb):
        pr = preps[bi]
        y = jnp.concatenate([y2[bi * npairs + pi][0:c] + y2[bi * npairs + pi][c:n2]
                             for pi in range(npairs)], axis=1)
        mean = head_sum(y) * inv_n
        d = y - mean
        var = head_sum(d * d) * inv_n
        yn = d * lax.rsqrt(var + RW_LN_EPS) * lw_ref[...] + lb_ref[...]
        bonus = head_sum(pr["r"] * pr["kmod"] * rk_ref[...]) * pr["v"]
        o_ref[bi] = (yn + bonus) * pr["g"]

    @pl.when(ci == pl.num_programs(1) - 1)
    def _():
        s_out_ref[...] = s_ref[...]


def _rwkv_mix(p_rw, shift0, s0_pairs, prm, c, t_valid):
    b, tp, cols = p_rw.shape
    width = prm["w0"].shape[1]
    npairs = width // LANES
    nb = RW_ROWS_PER_STEP if b % RW_ROWS_PER_STEP == 0 else 1
    assert tp % c == 0 and cols == 3 * width + 2 * LANES
    vec = lambda n: pl.BlockSpec((1, n), lambda i, j: (0, 0))
    mat = lambda r, n: pl.BlockSpec((r, n), lambda i, j: (0, 0))
    return pl.pallas_call(
        functools.partial(_rwkv_kernel, c=c, t_valid=t_valid, width=width),
        grid=(b // nb, tp // c),
        in_specs=[pl.BlockSpec((nb, c, cols), lambda i, j: (i, j, 0)),
                  pl.BlockSpec((nb, 1, cols), lambda i, j: (i, 0, 0)),
                  pl.BlockSpec((nb, npairs, LANES, LANES), lambda i, j: (i, 0, 0, 0)),
                  vec(cols), vec(width), mat(LANES, width), vec(width), mat(LANES, width),
                  mat(LANES, width), vec(width), vec(width), vec(width), vec(width), vec(width),
                  mat(LANES, LANES)],
        out_specs=[pl.BlockSpec((nb, c, width), lambda i, j: (i, j, 0)),
                   pl.BlockSpec((nb, npairs, LANES, LANES), lambda i, j: (i, 0, 0, 0))],
        out_shape=[jax.ShapeDtypeStruct((b, tp, width), F32),
                   jax.ShapeDtypeStruct((b, npairs, LANES, LANES), F32)],
        scratch_shapes=[pltpu.VMEM((nb, npairs, LANES, LANES), F32),
                        pltpu.VMEM((nb, SUBLANES, cols), F32)],
        compiler_params=_cparams(("parallel", "arbitrary"), 32),
        name="rwkv_mix",
    )(p_rw, shift0, s0_pairs, prm["mu"], prm["w0"], prm["w2"], prm["a0"], prm["a2"], prm["g2"],
      prm["kk"], prm["ka"], prm["rk"], prm["lnw"], prm["lnb"], prm["bd"])


def _lru_kernel(px_ref, pg_ref, conv0_ref, h0_ref, cw_ref, cb_ref, wa_ref, ba_ref, wx_ref, bx_ref,
                lam_ref, o_ref, hl_ref, ext_ref, h_ref, *, c, t_valid, width):
    ci = pl.program_id(1)
    npairs = width // LANES

    @pl.when(ci == 0)
    def _():
        ext_ref[0:SUBLANES, :] = conv0_ref[0]
        h_ref[0:1, :] = h0_ref[0]

    px = px_ref[0]
    ext_ref[SUBLANES:SUBLANES + c, :] = px
    acc = None
    for j in range(CONV_W - 1):
        term = cw_ref[j:j + 1, :] * ext_ref[SUBLANES - (CONV_W - 1) + j:SUBLANES - (CONV_W - 1) + j + c, :]
        acc = term if acc is None else acc + term
    xc = cb_ref[...] + (acc + cw_ref[CONV_W - 1:CONV_W, :] * px)
    ext_ref[0:SUBLANES, :] = ext_ref[c:c + SUBLANES, :]

    xcb = xc.astype(BF16)
    ga, gx = [], []
    for pi in range(npairs):
        sl = slice(pi * LANES, (pi + 1) * LANES)
        ga.append(_dot(xcb[:, sl], wa_ref[pi]))
        gx.append(_dot(xcb[:, sl], wx_ref[pi]))
    gate_r = jax.nn.sigmoid(jnp.concatenate(ga, axis=1) + ba_ref[...])
    gate_i = jax.nn.sigmoid(jnp.concatenate(gx, axis=1) + bx_ref[...])
    log_a = (-LRU_C * gate_r) * _softplus(-lam_ref[...])
    a_t = jnp.exp(log_a)
    b_t = jnp.sqrt(-jnp.tanh(log_a) * (a_t * a_t + 1.0)) * (gate_i * xc)

    row = lax.broadcasted_iota(jnp.int32, (c, 1), 0)
    b_t = b_t + jnp.where(row == 0, a_t * h_ref[0:1, :], 0.0)
    d = 1
    while d < c:
        keep = row >= d
        a_sh = jnp.where(keep, pltpu.roll(a_t, d, axis=0), 1.0)
        b_sh = jnp.where(keep, pltpu.roll(b_t, d, axis=0), 0.0)
        b_t = a_t * b_sh + b_t
        a_t = a_t * a_sh
        d *= 2
    hs = b_t
    h_ref[0:1, :] = hs[c - 1:c, :]
    o_ref[0] = hs * _gelu_tanh(pg_ref[0])

    last = t_valid - 1

    @pl.when(ci == last // c)
    def _():
        hl_ref[0] = hs[last % c:last % c + 1, :]


def _lru_mix(p_x, p_gate, conv0_pad, h0, prm, c, t_valid):
    b, tp, width = p_x.shape
    npairs = width // LANES
    assert tp % c == 0 and c >= SUBLANES
    vec = lambda n: pl.BlockSpec((1, n), lambda i, j: (0, 0))
    blk = pl.BlockSpec((npairs, LANES, LANES), lambda i, j: (0, 0, 0))
    return pl.pallas_call(
        functools.partial(_lru_kernel, c=c, t_valid=t_valid, width=width),
        grid=(b, tp // c),
        in_specs=[pl.BlockSpec((1, c, width), lambda i, j: (i, j, 0)),
                  pl.BlockSpec((1, c, width), lambda i, j: (i, j, 0)),
                  pl.BlockSpec((1, SUBLANES, width), lambda i, j: (i, 0, 0)),
                  pl.BlockSpec((1, 1, width), lambda i, j: (i, 0, 0)),
                  pl.BlockSpec((CONV_W, width), lambda i, j: (0, 0)),
                  vec(width), blk, vec(width), blk, vec(width), vec(width)],
        out_specs=[pl.BlockSpec((1, c, width), lambda i, j: (i, j, 0)),
                   pl.BlockSpec((1, 1, width), lambda i, j: (i, 0, 0))],
        out_shape=[jax.ShapeDtypeStruct((b, tp, width), F32),
                   jax.ShapeDtypeStruct((b, 1, width), F32)],
        scratch_shapes=[pltpu.VMEM((c + SUBLANES, width), F32),
                        pltpu.VMEM((SUBLANES, width), F32)],
        compiler_params=_cparams(("parallel", "arbitrary"), 32),
        name="lru_mix",
    )(p_x, p_gate, conv0_pad, h0, prm["cw"], prm["cb"], prm["wa"], prm["ba"], prm["wx"], prm["bx"],
      prm["lam"])


def _sb_weights(zs, cums, upper2, masks):
    sps = [jnp.maximum(z, 0.0) + jnp.log(1.0 + jnp.exp(-jnp.abs(z))) for z in zs]
    sps = [sp if m is None else jnp.where(m, sp, 0.0) for sp, m in zip(sps, masks)]
    suffix = [_dot(jnp.concatenate(_split2(sp), axis=1), upper2) for sp in sps]
    atts = [jnp.exp((z - c) - sf) for z, c, sf in zip(zs, cums, suffix)]
    atts = [a if m is None else jnp.where(m, a, 0.0) for a, m in zip(atts, masks)]
    cums = [c + sf[:, 0:1] for c, sf in zip(cums, suffix)]
    return atts, cums


def _upper2(n):
    jj = lax.broadcasted_iota(jnp.int32, (2 * n, n), 0) & (n - 1)
    ss = lax.broadcasted_iota(jnp.int32, (2 * n, n), 1)
    return jnp.where(jj >= ss, 1.0, 0.0).astype(BF16)


def _sb_prompt_kernel(bias_ref, q_ref, k_ref, v_ref, o_ref, z_ref, att_ref, acc_ref, cum_ref):
    pi = pl.program_id(1)
    qi = pl.program_id(2)
    blk = SB_TILE
    lane = lax.broadcasted_iota(jnp.int32, (1, LANES), 1)
    first = lane < SB_HD
    q = q_ref[0] * (SB_HD ** -0.5)
    qs = [jnp.where(first, q, 0.0).astype(BF16), jnp.where(first, 0.0, q).astype(BF16)]
    biases = [bias_ref[2 * pi], bias_ref[2 * pi + 1]]
    upper2 = _upper2(blk)

    def scores(kb):
        start = pl.multiple_of(kb * blk, blk)
        kblk = k_ref[0, pl.ds(start, blk), :].astype(BF16)
        return [_dot_nt(qh, kblk) + bh for qh, bh in zip(qs, biases)]

    def weighted_values(atts, kb):
        start = pl.multiple_of(kb * blk, blk)
        vblk = v_ref[0, pl.ds(start, blk), :].astype(BF16)
        return [_dot(a, vblk) for a in atts]

    qpos = lax.broadcasted_iota(jnp.int32, (blk, blk), 0)
    kpos = lax.broadcasted_iota(jnp.int32, (blk, blk), 1)
    mask = kpos < qpos
    zero = jnp.zeros((blk, 1), F32)
    heads2 = range(2)
    atts, cums = _sb_weights(scores(qi), [zero, zero], upper2, [mask, mask])
    zs = scores(jnp.maximum(qi - 1, 0))
    for h in heads2:
        att_ref[0, h] = atts[h].astype(BF16)
        z_ref[0, h] = zs[h]
        acc_ref[h] = jnp.zeros((blk, LANES), F32)
        cum_ref[h] = cums[h]

    def trip(kb, cur, nxt):
        zs_next = scores(jnp.maximum(kb - 1, 0))
        for h in heads2:
            z_ref[nxt, h] = zs_next[h]
        pvs = weighted_values([att_ref[cur, h] for h in heads2], kb + 1)
        for h in heads2:
            acc_ref[h] += pvs[h]
        atts, cums = _sb_weights([z_ref[cur, h] for h in heads2], [cum_ref[h] for h in heads2],
                                 upper2, [None, None])
        for h in heads2:
            att_ref[nxt, h] = atts[h].astype(BF16)
            cum_ref[h] = cums[h]

    def body(j, carry):
        kb = qi - 1 - 2 * j
        trip(kb, 0, 1)

        @pl.when(kb >= 1)
        def _():
            trip(kb - 1, 1, 0)

        return carry

    lax.fori_loop(0, (qi + 1) // 2, body, 0)
    last = qi & 1
    pvs = weighted_values([att_ref[last, h] for h in heads2], 0)
    o_ref[0] = jnp.where(first, acc_ref[0] + pvs[0], acc_ref[1] + pvs[1])


def _sb_prompt(qkv, bias, heads):
    b, t, w3 = qkv.shape
    w = w3 // 3
    npairs = w // LANES
    assert t % SB_TILE == 0 and heads * SB_HD == w
    return pl.pallas_call(
        _sb_prompt_kernel,
        grid=(b, npairs, t // SB_TILE),
        in_specs=[pl.BlockSpec(memory_space=pltpu.SMEM),
                  pl.BlockSpec((1, SB_TILE, LANES), lambda i, p, j: (i, j, p)),
                  pl.BlockSpec((1, t, LANES), lambda i, p, j: (i, 0, npairs + p)),
                  pl.BlockSpec((1, t, LANES), lambda i, p, j: (i, 0, 2 * npairs + p))],
        out_specs=pl.BlockSpec((1, SB_TILE, LANES), lambda i, p, j: (i, j, p)),
        out_shape=jax.ShapeDtypeStruct((b, t, w), F32),
        scratch_shapes=[pltpu.VMEM((2, 2, SB_TILE, SB_TILE), F32),
                        pltpu.VMEM((2, 2, SB_TILE, SB_TILE), BF16),
                        pltpu.VMEM((2, SB_TILE, LANES), F32),
                        pltpu.VMEM((2, SB_TILE, 1), F32)],
        compiler_params=_cparams(("parallel", "parallel", "arbitrary"), 32),
        name="sb_prompt",
    )(bias, qkv, qkv, qkv)


def _sb_decode_kernel(pt_ref, bias_ref, q_ref, *refs, pages_per_step, heads):
    k_refs = refs[:pages_per_step]
    v_refs = refs[pages_per_step:2 * pages_per_step]
    o_ref, acc_ref, cum_ref = refs[2 * pages_per_step:]
    j = pl.program_id(1)
    w = heads * SB_HD

    @pl.when(j == 0)
    def _():
        acc_ref[...] = jnp.zeros_like(acc_ref)
        cum_ref[...] = jnp.zeros_like(cum_ref)

    hrow = lax.broadcasted_iota(jnp.int32, (heads, w), 0)
    hcol = lax.broadcasted_iota(jnp.int32, (heads, w), 1) >> (SB_HD.bit_length() - 1)
    own = hrow == hcol
    qm = jnp.where(own, q_ref[0] * (SB_HD ** -0.5), 0.0).astype(BF16)
    upper2 = _upper2(PAGE)
    bias = bias_ref[...]

    pages = range(pages_per_step)
    zs = [_dot(qm, k_refs[i][...].astype(BF16)) + bias for i in pages]
    none = [None] * pages_per_step
    zero = jnp.zeros((heads, 1), F32)
    atts, tots = _sb_weights(zs, [zero] * pages_per_step, upper2, none)
    acc = acc_ref[...]
    cum = cum_ref[...]
    for i in pages:
        att = atts[i] * jnp.exp(-cum)
        acc = acc + _dot_nt(att.astype(BF16), v_refs[i][...].astype(BF16))
        cum = cum + tots[i]
    acc_ref[...] = acc
    cum_ref[...] = cum

    @pl.when(j == pl.num_programs(1) - 1)
    def _():
        o_ref[0] = jnp.sum(jnp.where(own, acc, 0.0), axis=0, keepdims=True)


def _sb_decode(q, k_t, v_t, page_table, bias, heads, pages_per_step):
    b, _, w = q.shape
    n_pages = page_table.shape[1]
    assert n_pages % pages_per_step == 0 and w == heads * SB_HD and k_t.shape[1] == PAGE
    steps = n_pages // pages_per_step

    def page_spec(i):
        return pl.BlockSpec((w, PAGE),
                            lambda bi, j, pt: (pt[bi, n_pages - 1 - (j * pages_per_step + i)], 0))

    return pl.pallas_call(
        functools.partial(_sb_decode_kernel, pages_per_step=pages_per_step, heads=heads),
        grid_spec=pltpu.PrefetchScalarGridSpec(
            num_scalar_prefetch=1,
            grid=(b, steps),
            in_specs=[pl.BlockSpec((heads, 1), lambda bi, j, pt: (0, 0)),
                      pl.BlockSpec((1, 1, w), lambda bi, j, pt: (bi, 0, 0))]
                     + [page_spec(i) for i in range(pages_per_step)] * 2,
            out_specs=pl.BlockSpec((1, 1, w), lambda bi, j, pt: (bi, 0, 0)),
            scratch_shapes=[pltpu.VMEM((heads, w), F32), pltpu.VMEM((heads, 1), F32)]),
        out_shape=jax.ShapeDtypeStruct((b, 1, w), F32),
        compiler_params=_cparams(("parallel", "arbitrary"), 48),
        name="sb_decode",
    )(page_table, bias.reshape(heads, 1), q, *([k_t] * pages_per_step), *([v_t] * pages_per_step))


def _pair_blockdiag(w):
    n, d, _ = w.shape
    w = w.reshape(n // 2, 2, d, d)
    z = jnp.zeros_like(w[:, 0])
    top = jnp.concatenate([w[:, 0], z], axis=2)
    bot = jnp.concatenate([z, w[:, 1]], axis=2)
    return jnp.concatenate([top, bot], axis=1)


def _state_to_pairs(s):
    b, h, n, _ = s.shape
    return _pair_blockdiag(s.reshape(b * h, n, n)).reshape(b, h // 2, 2 * n, 2 * n)


def _pairs_to_state(sp, n):
    b, hp = sp.shape[0], sp.shape[1]
    return jnp.stack([sp[:, :, :n, :n], sp[:, :, n:, n:]], axis=2).reshape(b, 2 * hp, n, n)


def _pad_rows(x, multiple):
    t = x.shape[1]
    t_pad = -(-t // multiple) * multiple
    return x if t_pad == t else jnp.pad(x, ((0, 0), (0, t_pad - t), (0, 0)))


def kernel(x_prompt, x_sample, state_rwkv_wkv, state_rwkv_shift, state_lru_h, state_lru_conv,
           cache_sb_k, cache_sb_v, page_table, cache_mem_k, cache_mem_v, mem_prompt,
           g_mix, g_mem, g_memkv, g_ffn, wq_mem, wk_mem, wv_mem, wo_mem, qn_mem, kn_mem,
           w_ffn_gate, w_ffn_up, w_ffn_down, w_in_ab, mu_shift, rw_w0, rw_w2, rw_a0, rw_a2,
           rw_g2, rw_kk, rw_ka, rw_rk, rw_lnx_w, rw_lnx_b, lru_conv_w, lru_conv_b, lru_wa,
           lru_ba, lru_wx, lru_bx, lru_lambda, w_out_ab, w_qkv_sb, w_out_sb, sb_bias):
    depth, d_model = g_mix.shape
    mem_heads = cache_mem_k.shape[3]
    mem_w = wq_mem.shape[2]
    rw_w = rw_w0.shape[1]
    rw_cols = mu_shift.shape[1]
    lru_w = lru_lambda.shape[1]
    rw_heads = rw_w // RW_HD
    sb_heads = sb_bias.shape[1]
    sb_w = sb_heads * SB_HD
    d_ff = w_ffn_gate.shape[2]
    lora_w = rw_w2.shape[1]
    lora_a = rw_a2.shape[1]
    assert lora_w + lora_a == LANES and rw_g2.shape[1] == LANES
    tf = 256
    assert d_ff % tf == 0
    bf = lambda z: z.astype(BF16)

    wg3 = [bf(w_ffn_gate[i]).reshape(d_model, d_ff // tf, tf).transpose(1, 0, 2) for i in range(depth)]
    wu3 = [bf(w_ffn_up[i]).reshape(d_model, d_ff // tf, tf).transpose(1, 0, 2) for i in range(depth)]
    wd3 = [bf(w_ffn_down[i]).reshape(d_ff // tf, tf, d_model) for i in range(depth)]
    head_id = jnp.arange(LANES) // RW_HD
    bd = (head_id[:, None] == head_id[None, :]).astype(BF16)

    def even_params(e):
        zw = jnp.zeros((lora_a, rw_w), F32)
        za = jnp.zeros((lora_w, rw_w), F32)
        row = lambda z: z.reshape(1, -1)
        rw = dict(mu=row(mu_shift[e]), w0=row(rw_w0[e]), w2=bf(jnp.concatenate([rw_w2[e], zw], axis=0)),
                  a0=row(rw_a0[e]), a2=bf(jnp.concatenate([za, rw_a2[e]], axis=0)), g2=bf(rw_g2[e]),
                  kk=row(rw_kk[e]), ka=row(rw_ka[e]), rk=row(rw_rk[e]), lnw=row(rw_lnx_w[e]),
                  lnb=row(rw_lnx_b[e]), bd=bd)
        lru = dict(cw=lru_conv_w[e], cb=row(lru_conv_b[e]), wa=bf(_pair_blockdiag(lru_wa[e])),
                   ba=row(lru_ba[e]), wx=bf(_pair_blockdiag(lru_wx[e])), bx=row(lru_bx[e]),
                   lam=row(lru_lambda[e]))
        return rw, lru

    def trunk(x, sample):
        bsz, t, _ = x.shape
        m = bsz * t
        tm = 512 if m % 512 == 0 else m
        out = {}
        xf = x.reshape(m, d_model)
        for i in range(depth):
            if i % 2 == 0:
                e = i // 2
                rw, lru = even_params(e)
                p_rw, p_x, p_gate = _norm_proj(xf, g_mix[i], bf(w_in_ab[e]),
                                               [rw_cols, rw_cols + lru_w], tm)
                p_rw = p_rw.reshape(bsz, t, rw_cols)
                p_x = p_x.reshape(bsz, t, lru_w)
                p_gate = p_gate.reshape(bsz, t, lru_w)
                if sample:
                    wkv0, shift0 = state_rwkv_wkv[e], state_rwkv_shift[e]
                    h0, conv0 = state_lru_h[e], state_lru_conv[e]
                else:
                    wkv0 = jnp.zeros((bsz, rw_heads, RW_HD, RW_HD), F32)
                    shift0 = jnp.zeros((bsz, rw_cols), F32)
                    h0 = jnp.zeros((bsz, lru_w), F32)
                    conv0 = jnp.zeros((bsz, CONV_W - 1, lru_w), F32)
                o_rw, s_pairs = _rwkv_mix(_pad_rows(p_rw, RW_CHUNK), shift0.reshape(bsz, 1, rw_cols),
                                          _state_to_pairs(wkv0), rw, RW_CHUNK, t)
                conv0_pad = jnp.pad(conv0, ((0, 0), (SUBLANES - (CONV_W - 1), 0), (0, 0)))
                px_pad = _pad_rows(p_x, BF16_ROWS)
                c_lru = LRU_CHUNK if px_pad.shape[1] % LRU_CHUNK == 0 else px_pad.shape[1]
                o_lru, h_last = _lru_mix(px_pad, _pad_rows(p_gate, BF16_ROWS), conv0_pad,
                                         h0.reshape(bsz, 1, lru_w), lru, c_lru, t)
                out.setdefault("wkv", []).append(_pairs_to_state(s_pairs, RW_HD))
                out.setdefault("shift", []).append(p_rw[:, t - 1])
                out.setdefault("lru_h", []).append(h_last[:, 0])
                out.setdefault("lru_conv", []).append(
                    jnp.concatenate([conv0, p_x], axis=1)[:, -(CONV_W - 1):])
                w_out = bf(w_out_ab[e])
                mix_in, mix_w = [o_rw[:, :t], o_lru[:, :t]], [w_out[:rw_w], w_out[rw_w:]]
            else:
                o = i // 2
                (qkv,) = _norm_proj(xf, g_mix[i], bf(w_qkv_sb[o]), [], tm)
                qkv = qkv.reshape(bsz, t, 3 * sb_w)
                if sample:
                    pool = cache_sb_k.shape[1]
                    k_t = jnp.transpose(cache_sb_k, (0, 1, 3, 4, 2)).reshape(-1, PAGE)
                    v_t = jnp.transpose(cache_sb_v, (0, 1, 3, 4, 2)).reshape(-1, PAGE)
                    att = _sb_decode(qkv[:, :, :sb_w], k_t, v_t, page_table + o * pool, sb_bias[o],
                                     sb_heads, DECODE_PAGES_PER_STEP)
                else:
                    att = _sb_prompt(qkv, sb_bias[o], sb_heads)
                out.setdefault("sb_k", []).append(qkv[:, :, sb_w:2 * sb_w].reshape(bsz, t, sb_heads, SB_HD))
                out.setdefault("sb_v", []).append(qkv[:, :, 2 * sb_w:].reshape(bsz, t, sb_heads, SB_HD))
                mix_in, mix_w = [att.reshape(bsz, t, sb_w)], [bf(w_out_sb[o])]
            if sample:
                mk = cache_mem_k[i].reshape(bsz, -1, mem_w)
                mv = cache_mem_v[i].reshape(bsz, -1, mem_w)
            else:
                mk, mv = _memory_kv(mem_prompt, g_memkv[i], bf(wk_mem[i]), bf(wv_mem[i]), kn_mem[i],
                                    mem_heads)
                out.setdefault("mem_k", []).append(mk.reshape(bsz, -1, mem_heads, mem_w // mem_heads))
                out.setdefault("mem_v", []).append(mv.reshape(bsz, -1, mem_heads, mem_w // mem_heads))
            x3 = _pad_rows(xf.reshape(bsz, t, d_model), BF16_ROWS)
            t3 = x3.shape[1]
            big = t3 % 512 == 0
            x3 = _memx_residual(x3, [_pad_rows(a, BF16_ROWS) for a in mix_in], mix_w, g_mem[i],
                                bf(wq_mem[i]), qn_mem[i], mk, mv, bf(wo_mem[i]), mem_heads,
                                512 if big else t3, 1 if big or bsz % MEMX_ROWS_PER_STEP else MEMX_ROWS_PER_STEP)
            xf = x3[:, :t].reshape(m, d_model)
            xf = _ffn_residual(xf, g_ffn[i], wg3[i], wu3[i], wd3[i], tm)
        return xf.reshape(bsz, t, d_model), out

    y_p, sp = trunk(x_prompt, False)
    y_s, ss = trunk(x_sample, True)
    st = lambda arrs: jnp.stack(arrs, axis=0)
    return (y_p, y_s, st(sp["wkv"]), st(ss["wkv"]), st(sp["shift"]), st(ss["shift"]),
            st(sp["lru_h"]), st(ss["lru_h"]), st(sp["lru_conv"]), st(ss["lru_conv"]),
            st(sp["sb_k"]), st(ss["sb_k"]), st(sp["sb_v"]), st(ss["sb_v"]),
            st(sp["mem_k"]), st(sp["mem_v"]))
```

```python
import functools
import math

import jax
import jax.numpy as jnp
from jax import lax
from jax.experimental import pallas as pl
from jax.experimental.pallas import tpu as pltpu

F32 = jnp.float32
BF16 = jnp.bfloat16

NORM_EPS = 1e-6
RW_HD = 64
LANES = 128
SUBLANES = 8
BF16_ROWS = 16
RW_CHUNK = 64
RW_ROWS_PER_STEP = 4
SB_PAIRS_PER_STEP = 4
MEMX_ROWS_PER_STEP = 8
DECODE_PAGES_PER_STEP = 8
LRU_CHUNK = 256
RW_LN_EPS = RW_HD * 1e-5
LRU_C = 8.0
LRU_BD = 64
CONV_W = 4
SB_HD = 64
PAGE = 128
SB_TILE = 256
EXP_M05 = math.exp(-0.5)
VMEM_BYTES_V7X = 64 * 1024 * 1024


def _cparams(sem, vmem_mb):
    assert vmem_mb * 1024 * 1024 < VMEM_BYTES_V7X
    return pltpu.CompilerParams(dimension_semantics=sem, vmem_limit_bytes=vmem_mb * 1024 * 1024)


def _dot(a, b):
    return jnp.dot(a, b, preferred_element_type=F32)


def _dot_nt(a, b):
    return lax.dot_general(a, b, (((1,), (1,)), ((), ())), preferred_element_type=F32)


def _dot_tn(a, b):
    return lax.dot_general(a, b, (((0,), (0,)), ((), ())), preferred_element_type=F32)


def _split2(x):
    hi = x.astype(BF16)
    lo = (x - hi.astype(F32)).astype(BF16)
    return hi, lo


def _split3(x):
    hi = x.astype(BF16)
    r = x - hi.astype(F32)
    mid = r.astype(BF16)
    lo = (r - mid.astype(F32)).astype(BF16)
    return hi, mid, lo


def _dot_xw(x, w_bf16, parts=2):
    pieces = _split2(x) if parts == 2 else _split3(x)
    out = _dot(pieces[0], w_bf16)
    for p in pieces[1:]:
        out = out + _dot(p, w_bf16)
    return out


def _dot_wx(w_bf16, x, parts=3):
    pieces = _split2(x) if parts == 2 else _split3(x)
    out = _dot(w_bf16, pieces[0])
    for p in pieces[1:]:
        out = out + _dot(w_bf16, p)
    return out


def _mm3(a, b, dot):
    ah, al = _split2(a)
    bh, bl = _split2(b)
    return dot(ah, bh) + (dot(al, bh) + dot(ah, bl))


def _mm1(a, b, dot):
    return dot(a.astype(BF16), b.astype(BF16))


def _softplus(x):
    return jnp.maximum(x, 0.0) + jnp.log1p(jnp.exp(-jnp.abs(x)))


def _gelu_tanh(x):
    return 0.5 * x * (1.0 + jnp.tanh(math.sqrt(2.0 / math.pi) * (x + 0.044715 * (x * x * x))))


def _rms(x, g, eps=NORM_EPS):
    ms = jnp.mean(x * x, axis=-1, keepdims=True)
    return x * lax.rsqrt(ms + eps) * g


def _norm_proj_kernel(x_ref, g_ref, w_ref, *o_refs, splits, chunk):
    h = _rms(x_ref[...], g_ref[...]).astype(BF16)
    for o_ref, (s, e) in zip(o_refs, splits):
        for c0 in range(s, e, chunk):
            c1 = min(c0 + chunk, e)
            o_ref[:, c0 - s:c1 - s] = _dot(h, w_ref[:, c0:c1])


def _norm_proj(x, g, w_bf16, splits, tm):
    m, d = x.shape
    n = w_bf16.shape[1]
    assert m % tm == 0
    bounds = [0] + list(splits) + [n]
    ranges = [(bounds[i], bounds[i + 1]) for i in range(len(bounds) - 1)]
    out_shape = [jax.ShapeDtypeStruct((m, e - s), F32) for s, e in ranges]
    out_specs = [pl.BlockSpec((tm, e - s), lambda i: (i, 0)) for s, e in ranges]
    return pl.pallas_call(
        functools.partial(_norm_proj_kernel, splits=ranges, chunk=512),
        grid=(m // tm,),
        in_specs=[pl.BlockSpec((tm, d), lambda i: (i, 0)),
                  pl.BlockSpec((1, d), lambda i: (0, 0)),
                  pl.BlockSpec((d, n), lambda i: (0, 0))],
        out_specs=out_specs,
        out_shape=out_shape,
        compiler_params=_cparams(("parallel",), 48),
        name="norm_proj",
    )(x, g.reshape(1, d), w_bf16)


def _ffn_kernel(x_ref, g_ref, wg_ref, wu_ref, wd_ref, o_ref, h_ref):
    x = x_ref[...]
    h_ref[...] = _rms(x, g_ref[...]).astype(BF16)
    o_ref[...] = x

    def body(j, carry):
        h = h_ref[...]
        gate = _dot(h, wg_ref[j])
        up = _dot(h, wu_ref[j])
        act = (gate * jax.nn.sigmoid(gate) * up).astype(BF16)
        o_ref[...] += _dot(act, wd_ref[j])
        return carry

    lax.fori_loop(0, wg_ref.shape[0], body, 0)


def _ffn_residual(x, g, wg3, wu3, wd3, tm):
    m, d = x.shape
    nf, _, tf = wg3.shape
    assert m % tm == 0
    return pl.pallas_call(
        _ffn_kernel,
        grid=(m // tm,),
        in_specs=[pl.BlockSpec((tm, d), lambda i: (i, 0)),
                  pl.BlockSpec((1, d), lambda i: (0, 0)),
                  pl.BlockSpec((nf, d, tf), lambda i: (0, 0, 0)),
                  pl.BlockSpec((nf, d, tf), lambda i: (0, 0, 0)),
                  pl.BlockSpec((nf, tf, d), lambda i: (0, 0, 0))],
        out_specs=pl.BlockSpec((tm, d), lambda i: (i, 0)),
        out_shape=jax.ShapeDtypeStruct((m, d), F32),
        scratch_shapes=[pltpu.VMEM((tm, d), BF16)],
        compiler_params=_cparams(("parallel",), 56),
        name="ffn_residual",
    )(x, g.reshape(1, d), wg3, wu3, wd3)


def _memkv_kernel(m_ref, g_ref, wk_ref, wv_ref, kn_ref, k_ref, v_ref, *, heads, hd):
    mn = _rms(m_ref[0], g_ref[...]).astype(BF16)
    k = _dot(mn, wk_ref[...])
    v_ref[0] = _dot(mn, wv_ref[...])
    for hh in range(heads):
        sl = slice(hh * hd, (hh + 1) * hd)
        k_ref[0, :, sl] = _rms(k[:, sl], kn_ref[...])


def _memory_kv(mem, g, wk, wv, kn, heads):
    b, nm, d = mem.shape
    w = wk.shape[1]
    hd = w // heads
    return pl.pallas_call(
        functools.partial(_memkv_kernel, heads=heads, hd=hd),
        grid=(b,),
        in_specs=[pl.BlockSpec((1, nm, d), lambda i: (i, 0, 0)),
                  pl.BlockSpec((1, d), lambda i: (0, 0)),
                  pl.BlockSpec((d, w), lambda i: (0, 0)),
                  pl.BlockSpec((d, w), lambda i: (0, 0)),
                  pl.BlockSpec((1, hd), lambda i: (0, 0))],
        out_specs=[pl.BlockSpec((1, nm, w), lambda i: (i, 0, 0)),
                   pl.BlockSpec((1, nm, w), lambda i: (i, 0, 0))],
        out_shape=[jax.ShapeDtypeStruct((b, nm, w), F32)] * 2,
        compiler_params=_cparams(("parallel",), 32),
        name="memory_kv",
    )(mem, g.reshape(1, d), wk, wv, kn.reshape(1, hd))


def _memx_kernel(*refs, n_proj, heads, hd):
    x_ref, a_refs, w_refs = refs[0], refs[1:1 + n_proj], refs[1 + n_proj:1 + 2 * n_proj]
    g_ref, wq_ref, qn_ref, k_ref, v_ref, wo_ref, o_ref = refs[1 + 2 * n_proj:]
    nb = x_ref.shape[0]
    scale = hd ** -0.5
    x1 = []
    for bi in range(nb):
        x = x_ref[bi]
        for a_ref, w_ref in zip(a_refs, w_refs):
            x = x + _dot(a_ref[bi].astype(BF16), w_ref[...])
        x1.append(x)
    qs = [_dot(_rms(x, g_ref[...]).astype(BF16), wq_ref[...]) for x in x1]
    chains = [(bi, slice(hh * hd, (hh + 1) * hd)) for bi in range(nb) for hh in range(heads)]
    qh = [_rms(qs[bi][:, sl], qn_ref[...]).astype(BF16) for bi, sl in chains]
    s = [_dot_nt(qh[c], k_ref[bi, :, sl].astype(BF16)) * scale for c, (bi, sl) in enumerate(chains)]
    e = [jnp.exp(z - jnp.max(z, axis=-1, keepdims=True)) for z in s]
    p = [(z / jnp.sum(z, axis=-1, keepdims=True)).astype(BF16) for z in e]
    oh = [_dot(p[c], v_ref[bi, :, sl].astype(BF16)) for c, (bi, sl) in enumerate(chains)]
    for bi in range(nb):
        o = jnp.concatenate(oh[bi * heads:(bi + 1) * heads], axis=1).astype(BF16)
        o_ref[bi] = x1[bi] + _dot(o, wo_ref[...])


def _memx_residual(x, a_list, w_list, g, wq, qn, k, v, wo, heads, tm, nb):
    b, t, d = x.shape
    nm, w = k.shape[1], k.shape[2]
    hd = w // heads
    assert t % tm == 0 and b % nb == 0
    rows = lambda n: pl.BlockSpec((nb, tm, n), lambda i, j: (i, j, 0))
    full = lambda arr: pl.BlockSpec(arr.shape, lambda i, j: (0,) * arr.ndim)
    mem = pl.BlockSpec((nb, nm, w), lambda i, j: (i, 0, 0))
    g2, qn2 = g.reshape(1, d), qn.reshape(1, hd)
    return pl.pallas_call(
        functools.partial(_memx_kernel, n_proj=len(a_list), heads=heads, hd=hd),
        grid=(b // nb, t // tm),
        in_specs=[rows(d)] + [rows(a.shape[2]) for a in a_list] + [full(wi) for wi in w_list]
                 + [full(g2), full(wq), full(qn2), mem, mem, full(wo)],
        out_specs=rows(d),
        out_shape=jax.ShapeDtypeStruct((b, t, d), F32),
        compiler_params=_cparams(("parallel", "parallel"), 40),
        name="memx_residual",
    )(x, *a_list, *w_list, g2, wq, qn2, k, v, wo)


def _unit_lower_inverse(lmats, n, block):
    row = lax.broadcasted_iota(jnp.int32, (n, n), 0)
    col = lax.broadcasted_iota(jnp.int32, (n, n), 1)
    eye = jnp.where(row == col, 1.0, 0.0).astype(F32)
    first = ((row ^ col) < 2) & ((row & 1) != 0) & ((col & 1) == 0)
    ts = [eye - jnp.where(first, l, 0.0) for l in lmats]
    s = 2
    while s < block:
        lower_left = ((row ^ col) < 2 * s) & ((row & s) != 0) & ((col & s) == 0)
        us = [_mm1(jnp.where(lower_left, l, 0.0), t, _dot) for l, t in zip(lmats, ts)]
        ts = [t - _mm1(t, u, _dot) for t, u in zip(ts, us)]
        s *= 2
    return ts


def _rwkv_kernel(p_ref, sh0_ref, s0_ref, mu_ref, w0_ref, w2_ref, a0_ref, a2_ref, g2_ref,
                 kk_ref, ka_ref, rk_ref, lw_ref, lb_ref, bd_ref,
                 o_ref, s_out_ref, s_ref, prev_ref, *, c, t_valid, width):
    ci = pl.program_id(1)
    nb = p_ref.shape[0]
    npairs = width // LANES
    sls = [slice(pi * LANES, (pi + 1) * LANES) for pi in range(npairs)]

    @pl.when(ci == 0)
    def _():
        s_ref[...] = s0_ref[...]
        prev_ref[:, 0:1, :] = sh0_ref[...]

    bd = bd_ref[...]

    def head_sum(z):
        rows = jnp.concatenate([z[:, sl] for sl in sls], axis=0)
        tot = _dot_xw(rows, bd)
        return jnp.concatenate([tot[i * c:(i + 1) * c] for i in range(npairs)], axis=1)

    row1 = lax.broadcasted_iota(jnp.int32, (c, 1), 0)
    rr = lax.broadcasted_iota(jnp.int32, (c, c), 0)
    cc = lax.broadcasted_iota(jnp.int32, (c, c), 1)
    tril = jnp.where(cc <= rr, 1.0, 0.0).astype(BF16)

    def prepare(bi):
        p = p_ref[bi]
        prev = jnp.where(row1 == 0, prev_ref[bi, 0:1, :], pltpu.roll(p, 1, axis=0))
        prev_ref[bi, 0:1, :] = p[c - 1:c, :]
        xs = p + mu_ref[...] * (prev - p)
        r = xs[:, 0:width]
        k = xs[:, width:2 * width]
        v = xs[:, 2 * width:3 * width]
        dwa = xs[:, 3 * width:3 * width + LANES]
        dg = xs[:, 3 * width + LANES:3 * width + 2 * LANES]
        u = w0_ref[...] + _dot(jnp.tanh(dwa).astype(BF16), w2_ref[...])
        logw = -EXP_M05 * jax.nn.sigmoid(u)
        a = jax.nn.sigmoid(a0_ref[...] + _dot(dwa.astype(BF16), a2_ref[...]))
        g = _dot(jax.nn.sigmoid(dg).astype(BF16), g2_ref[...])
        kkr = k * kk_ref[...]
        kk = kkr / jnp.maximum(jnp.sqrt(head_sum(kkr * kkr)), 1e-12)
        kmod = k * (1.0 + (a - 1.0) * ka_ref[...])
        if t_valid % c != 0:
            valid = ((ci * c + row1) < t_valid).astype(F32)
            logw, kk, kmod, v = logw * valid, kk * valid, kmod * valid, v * valid
        b = kk * a
        lc = _dot_wx(tril, logw, parts=3)
        lc_end = lc[c - 1:c, :]
        inv_c = jnp.exp(-lc)
        to_end = jnp.exp(lc_end - lc)
        return dict(r=r, v=v, kmod=kmod, g=g, kk_t=kk * jnp.exp(lc - logw), b_t=b * inv_c,
                    k_t=kmod * inv_c, r_t=r * jnp.exp(lc), b_end=b * to_end, k_end=kmod * to_end,
                    c_end=jnp.exp(lc_end))

    preps = [prepare(bi) for bi in range(nb)]

    n2 = 2 * c
    lane = lax.broadcasted_iota(jnp.int32, (1, LANES), 1)
    m0 = (lane < RW_HD).astype(F32)
    m1 = 1.0 - m0
    r2 = lax.broadcasted_iota(jnp.int32, (n2, n2), 0)
    c2 = lax.broadcasted_iota(jnp.int32, (n2, n2), 1)
    strict = c2 < r2
    incl = c2 <= r2

    def stack(z):
        return jnp.concatenate([z * m0, z * m1], axis=0).astype(BF16)

    chains = [(bi, pi) for bi in range(nb) for pi in range(npairs)]
    ids = range(len(chains))
    stk = lambda name: [stack(preps[bi][name][:, sls[pi]]) for bi, pi in chains]
    kks, bs, ks, rs, vs = stk("kk_t"), stk("b_t"), stk("k_t"), stk("r_t"), stk("v")
    s_old = [s_ref[bi, pi] for bi, pi in chains]
    s_bf = [s.astype(BF16) for s in s_old]
    a_kb = [jnp.where(strict, _dot_nt(kks[i], bs[i]), 0.0) for i in ids]
    a_kk = [jnp.where(strict, _dot_nt(kks[i], ks[i]), 0.0).astype(BF16) for i in ids]
    rhs = [-(_dot_nt(kks[i], s_bf[i]) + _dot(a_kk[i], vs[i])) for i in ids]
    tinv = _unit_lower_inverse(a_kb, n2, c)
    sas = [_mm1(tinv[i], rhs[i], _dot).astype(BF16) for i in ids]
    a_rb = [jnp.where(incl, _dot_nt(rs[i], bs[i]), 0.0).astype(BF16) for i in ids]
    a_rk = [jnp.where(incl, _dot_nt(rs[i], ks[i]), 0.0).astype(BF16) for i in ids]
    y2 = [_dot_nt(rs[i], s_bf[i]) + _dot(a_rb[i], sas[i]) + _dot(a_rk[i], vs[i]) for i in ids]
    for i, (bi, pi) in enumerate(chains):
        s_ref[bi, pi] = (s_old[i] * preps[bi]["c_end"][:, sls[pi]]
                         + _dot_tn(sas[i], stack(preps[bi]["b_end"][:, sls[pi]]))
                         + _dot_tn(vs[i], stack(preps[bi]["k_end"][:, sls[pi]])))

    inv_n = 1.0 / RW_HD
    for bi in range(nb):
        pr = preps[bi]
        y = jnp.concatenate([y2[bi * npairs + pi][0:c] + y2[bi * npairs + pi][c:n2]
                             for pi in range(npairs)], axis=1)
        mean = head_sum(y) * inv_n
        d = y - mean
        var = head_sum(d * d) * inv_n
        yn = d * lax.rsqrt(var + RW_LN_EPS) * lw_ref[...] + lb_ref[...]
        bonus = head_sum(pr["r"] * pr["kmod"] * rk_ref[...]) * pr["v"]
        o_ref[bi] = (yn + bonus) * pr["g"]

    @pl.when(ci == pl.num_programs(1) - 1)
    def _():
        s_out_ref[...] = s_ref[...]


def _rwkv_mix(p_rw, shift0, s0_pairs, prm, c, t_valid):
    b, tp, cols = p_rw.shape
    width = prm["w0"].shape[1]
    npairs = width // LANES
    nb = RW_ROWS_PER_STEP if b % RW_ROWS_PER_STEP == 0 else 1
    assert tp % c == 0 and cols == 3 * width + 2 * LANES
    vec = lambda n: pl.BlockSpec((1, n), lambda i, j: (0, 0))
    mat = lambda r, n: pl.BlockSpec((r, n), lambda i, j: (0, 0))
    return pl.pallas_call(
        functools.partial(_rwkv_kernel, c=c, t_valid=t_valid, width=width),
        grid=(b // nb, tp // c),
        in_specs=[pl.BlockSpec((nb, c, cols), lambda i, j: (i, j, 0)),
                  pl.BlockSpec((nb, 1, cols), lambda i, j: (i, 0, 0)),
                  pl.BlockSpec((nb, npairs, LANES, LANES), lambda i, j: (i, 0, 0, 0)),
                  vec(cols), vec(width), mat(LANES, width), vec(width), mat(LANES, width),
                  mat(LANES, width), vec(width), vec(width), vec(width), vec(width), vec(width),
                  mat(LANES, LANES)],
        out_specs=[pl.BlockSpec((nb, c, width), lambda i, j: (i, j, 0)),
                   pl.BlockSpec((nb, npairs, LANES, LANES), lambda i, j: (i, 0, 0, 0))],
        out_shape=[jax.ShapeDtypeStruct((b, tp, width), F32),
                   jax.ShapeDtypeStruct((b, npairs, LANES, LANES), F32)],
        scratch_shapes=[pltpu.VMEM((nb, npairs, LANES, LANES), F32),
                        pltpu.VMEM((nb, SUBLANES, cols), F32)],
        compiler_params=_cparams(("parallel", "arbitrary"), 32),
        name="rwkv_mix",
    )(p_rw, shift0, s0_pairs, prm["mu"], prm["w0"], prm["w2"], prm["a0"], prm["a2"], prm["g2"],
      prm["kk"], prm["ka"], prm["rk"], prm["lnw"], prm["lnb"], prm["bd"])


def _lru_kernel(px_ref, pg_ref, conv0_ref, h0_ref, cw_ref, cb_ref, wa_ref, ba_ref, wx_ref, bx_ref,
                lam_ref, o_ref, hl_ref, ext_ref, h_ref, *, c, t_valid, width):
    ci = pl.program_id(1)
    npairs = width // LANES

    @pl.when(ci == 0)
    def _():
        ext_ref[0:SUBLANES, :] = conv0_ref[0]
        h_ref[0:1, :] = h0_ref[0]

    px = px_ref[0]
    ext_ref[SUBLANES:SUBLANES + c, :] = px
    acc = None
    for j in range(CONV_W - 1):
        term = cw_ref[j:j + 1, :] * ext_ref[SUBLANES - (CONV_W - 1) + j:SUBLANES - (CONV_W - 1) + j + c, :]
        acc = term if acc is None else acc + term
    xc = cb_ref[...] + (acc + cw_ref[CONV_W - 1:CONV_W, :] * px)
    ext_ref[0:SUBLANES, :] = ext_ref[c:c + SUBLANES, :]

    xcb = xc.astype(BF16)
    ga, gx = [], []
    for pi in range(npairs):
        sl = slice(pi * LANES, (pi + 1) * LANES)
        ga.append(_dot(xcb[:, sl], wa_ref[pi]))
        gx.append(_dot(xcb[:, sl], wx_ref[pi]))
    gate_r = jax.nn.sigmoid(jnp.concatenate(ga, axis=1) + ba_ref[...])
    gate_i = jax.nn.sigmoid(jnp.concatenate(gx, axis=1) + bx_ref[...])
    log_a = (-LRU_C * gate_r) * _softplus(-lam_ref[...])
    a_t = jnp.exp(log_a)
    b_t = jnp.sqrt(-jnp.tanh(log_a) * (a_t * a_t + 1.0)) * (gate_i * xc)

    row = lax.broadcasted_iota(jnp.int32, (c, 1), 0)
    b_t = b_t + jnp.where(row == 0, a_t * h_ref[0:1, :], 0.0)
    d = 1
    while d < c:
        keep = row >= d
        a_sh = jnp.where(keep, pltpu.roll(a_t, d, axis=0), 1.0)
        b_sh = jnp.where(keep, pltpu.roll(b_t, d, axis=0), 0.0)
        b_t = a_t * b_sh + b_t
        a_t = a_t * a_sh
        d *= 2
    hs = b_t
    h_ref[0:1, :] = hs[c - 1:c, :]
    o_ref[0] = hs * _gelu_tanh(pg_ref[0])

    last = t_valid - 1

    @pl.when(ci == last // c)
    def _():
        hl_ref[0] = hs[last % c:last % c + 1, :]


def _lru_mix(p_x, p_gate, conv0_pad, h0, prm, c, t_valid):
    b, tp, width = p_x.shape
    npairs = width // LANES
    assert tp % c == 0 and c >= SUBLANES
    vec = lambda n: pl.BlockSpec((1, n), lambda i, j: (0, 0))
    blk = pl.BlockSpec((npairs, LANES, LANES), lambda i, j: (0, 0, 0))
    return pl.pallas_call(
        functools.partial(_lru_kernel, c=c, t_valid=t_valid, width=width),
        grid=(b, tp // c),
        in_specs=[pl.BlockSpec((1, c, width), lambda i, j: (i, j, 0)),
                  pl.BlockSpec((1, c, width), lambda i, j: (i, j, 0)),
                  pl.BlockSpec((1, SUBLANES, width), lambda i, j: (i, 0, 0)),
                  pl.BlockSpec((1, 1, width), lambda i, j: (i, 0, 0)),
                  pl.BlockSpec((CONV_W, width), lambda i, j: (0, 0)),
                  vec(width), blk, vec(width), blk, vec(width), vec(width)],
        out_specs=[pl.BlockSpec((1, c, width), lambda i, j: (i, j, 0)),
                   pl.BlockSpec((1, 1, width), lambda i, j: (i, 0, 0))],
        out_shape=[jax.ShapeDtypeStruct((b, tp, width), F32),
                   jax.ShapeDtypeStruct((b, 1, width), F32)],
        scratch_shapes=[pltpu.VMEM((c + SUBLANES, width), F32),
                        pltpu.VMEM((SUBLANES, width), F32)],
        compiler_params=_cparams(("parallel", "arbitrary"), 32),
        name="lru_mix",
    )(p_x, p_gate, conv0_pad, h0, prm["cw"], prm["cb"], prm["wa"], prm["ba"], prm["wx"], prm["bx"],
      prm["lam"])


def _sb_weights(zs, cums, upper2, masks):
    sps = [jnp.maximum(z, 0.0) + jnp.log(1.0 + jnp.exp(-jnp.abs(z))) for z in zs]
    sps = [sp if m is None else jnp.where(m, sp, 0.0) for sp, m in zip(sps, masks)]
    suffix = [_dot(jnp.concatenate(_split2(sp), axis=1), upper2) for sp in sps]
    atts = [jnp.exp((z - c) - sf) for z, c, sf in zip(zs, cums, suffix)]
    atts = [a if m is None else jnp.where(m, a, 0.0) for a, m in zip(atts, masks)]
    cums = [c + sf[:, 0:1] for c, sf in zip(cums, suffix)]
    return atts, cums


def _upper2(n):
    jj = lax.broadcasted_iota(jnp.int32, (2 * n, n), 0) & (n - 1)
    ss = lax.broadcasted_iota(jnp.int32, (2 * n, n), 1)
    return jnp.where(jj >= ss, 1.0, 0.0).astype(BF16)


def _sb_prompt_kernel(bias_ref, q_ref, k_ref, v_ref, o_ref, z_ref, att_ref, acc_ref, cum_ref):
    pi = pl.program_id(1)
    qi = pl.program_id(2)
    blk = SB_TILE
    npairs = q_ref.shape[2] // LANES
    nheads = 2 * npairs
    heads = range(nheads)
    sls = [slice(p * LANES, (p + 1) * LANES) for p in range(npairs)]
    lane = lax.broadcasted_iota(jnp.int32, (1, LANES), 1)
    first = lane < SB_HD
    q = q_ref[0] * (SB_HD ** -0.5)
    qs = [jnp.where(first == (h % 2 == 0), q[:, sls[h // 2]], 0.0).astype(BF16) for h in heads]
    biases = [bias_ref[nheads * pi + h] for h in heads]
    upper2 = _upper2(blk)

    def scores(kb):
        start = pl.multiple_of(kb * blk, blk)
        kblk = [k_ref[0, pl.ds(start, blk), sl].astype(BF16) for sl in sls]
        return [_dot_nt(qs[h], kblk[h // 2]) + biases[h] for h in heads]

    def weighted_values(atts, kb):
        start = pl.multiple_of(kb * blk, blk)
        vblk = [v_ref[0, pl.ds(start, blk), sl].astype(BF16) for sl in sls]
        return [_dot(atts[h], vblk[h // 2]) for h in heads]

    qpos = lax.broadcasted_iota(jnp.int32, (blk, blk), 0)
    kpos = lax.broadcasted_iota(jnp.int32, (blk, blk), 1)
    mask = kpos < qpos
    zero = jnp.zeros((blk, 1), F32)
    atts, cums = _sb_weights(scores(qi), [zero] * nheads, upper2, [mask] * nheads)
    zs = scores(jnp.maximum(qi - 1, 0))
    for h in heads:
        att_ref[0, h] = atts[h].astype(BF16)
        z_ref[0, h] = zs[h]
        acc_ref[h] = jnp.zeros((blk, LANES), F32)
        cum_ref[h] = cums[h]

    def trip(kb, cur, nxt):
        zs_next = scores(jnp.maximum(kb - 1, 0))
        for h in heads:
            z_ref[nxt, h] = zs_next[h]
        pvs = weighted_values([att_ref[cur, h] for h in heads], kb + 1)
        for h in heads:
            acc_ref[h] += pvs[h]
        atts, cums = _sb_weights([z_ref[cur, h] for h in heads], [cum_ref[h] for h in heads],
                                 upper2, [None] * nheads)
        for h in heads:
            att_ref[nxt, h] = atts[h].astype(BF16)
            cum_ref[h] = cums[h]

    def body(j, carry):
        kb = qi - 1 - 2 * j
        trip(kb, 0, 1)

        @pl.when(kb >= 1)
        def _():
            trip(kb - 1, 1, 0)

        return carry

    lax.fori_loop(0, (qi + 1) // 2, body, 0)
    last = qi & 1
    pvs = weighted_values([att_ref[last, h] for h in heads], 0)
    outs = [acc_ref[h] + pvs[h] for h in heads]
    o_ref[0] = jnp.concatenate([jnp.where(first, outs[2 * p], outs[2 * p + 1]) for p in range(npairs)],
                               axis=1)


def _sb_prompt(q, k, v, bias, heads):
    b, t, w = q.shape
    npairs = w // LANES
    pp = SB_PAIRS_PER_STEP if npairs % SB_PAIRS_PER_STEP == 0 else 1
    ngroups = npairs // pp
    gw = pp * LANES
    nh = 2 * pp
    assert t % SB_TILE == 0 and heads * SB_HD == w
    return pl.pallas_call(
        _sb_prompt_kernel,
        grid=(b, ngroups, t // SB_TILE),
        in_specs=[pl.BlockSpec(memory_space=pltpu.SMEM),
                  pl.BlockSpec((1, SB_TILE, gw), lambda i, p, j: (i, j, p)),
                  pl.BlockSpec((1, t, gw), lambda i, p, j: (i, 0, p)),
                  pl.BlockSpec((1, t, gw), lambda i, p, j: (i, 0, p))],
        out_specs=pl.BlockSpec((1, SB_TILE, gw), lambda i, p, j: (i, j, p)),
        out_shape=jax.ShapeDtypeStruct((b, t, w), F32),
        scratch_shapes=[pltpu.VMEM((2, nh, SB_TILE, SB_TILE), F32),
                        pltpu.VMEM((2, nh, SB_TILE, SB_TILE), BF16),
                        pltpu.VMEM((nh, SB_TILE, LANES), F32),
                        pltpu.VMEM((nh, SB_TILE, 1), F32)],
        compiler_params=_cparams(("parallel", "parallel", "arbitrary"), 40),
        name="sb_prompt",
    )(bias, q, k, v)


def _sb_decode_kernel(pt_ref, bias_ref, q_ref, *refs, pages_per_step, heads):
    k_refs = refs[:pages_per_step]
    v_refs = refs[pages_per_step:2 * pages_per_step]
    o_ref, acc_ref, cum_ref = refs[2 * pages_per_step:]
    j = pl.program_id(1)
    w = heads * SB_HD

    @pl.when(j == 0)
    def _():
        acc_ref[...] = jnp.zeros_like(acc_ref)
        cum_ref[...] = jnp.zeros_like(cum_ref)

    hrow = lax.broadcasted_iota(jnp.int32, (heads, w), 0)
    hcol = lax.broadcasted_iota(jnp.int32, (heads, w), 1) >> (SB_HD.bit_length() - 1)
    own = hrow == hcol
    qm = jnp.where(own, q_ref[0] * (SB_HD ** -0.5), 0.0).astype(BF16)
    upper2 = _upper2(PAGE)
    bias = bias_ref[...]

    pages = range(pages_per_step)
    zs = [_dot(qm, k_refs[i][...].astype(BF16)) + bias for i in pages]
    none = [None] * pages_per_step
    zero = jnp.zeros((heads, 1), F32)
    atts, tots = _sb_weights(zs, [zero] * pages_per_step, upper2, none)
    acc = acc_ref[...]
    cum = cum_ref[...]
    for i in pages:
        att = atts[i] * jnp.exp(-cum)
        acc = acc + _dot_nt(att.astype(BF16), v_refs[i][...].astype(BF16))
        cum = cum + tots[i]
    acc_ref[...] = acc
    cum_ref[...] = cum

    @pl.when(j == pl.num_programs(1) - 1)
    def _():
        o_ref[0] = jnp.sum(jnp.where(own, acc, 0.0), axis=0, keepdims=True)


def _sb_decode(q, k_t, v_t, page_table, bias, heads, pages_per_step):
    b, _, w = q.shape
    n_pages = page_table.shape[1]
    assert n_pages % pages_per_step == 0 and w == heads * SB_HD and k_t.shape[1] == PAGE
    steps = n_pages // pages_per_step

    def page_spec(i):
        return pl.BlockSpec((w, PAGE),
                            lambda bi, j, pt: (pt[bi, n_pages - 1 - (j * pages_per_step + i)], 0))

    return pl.pallas_call(
        functools.partial(_sb_decode_kernel, pages_per_step=pages_per_step, heads=heads),
        grid_spec=pltpu.PrefetchScalarGridSpec(
            num_scalar_prefetch=1,
            grid=(b, steps),
            in_specs=[pl.BlockSpec((heads, 1), lambda bi, j, pt: (0, 0)),
                      pl.BlockSpec((1, 1, w), lambda bi, j, pt: (bi, 0, 0))]
                     + [page_spec(i) for i in range(pages_per_step)] * 2,
            out_specs=pl.BlockSpec((1, 1, w), lambda bi, j, pt: (bi, 0, 0)),
            scratch_shapes=[pltpu.VMEM((heads, w), F32), pltpu.VMEM((heads, 1), F32)]),
        out_shape=jax.ShapeDtypeStruct((b, 1, w), F32),
        compiler_params=_cparams(("parallel", "arbitrary"), 48),
        name="sb_decode",
    )(page_table, bias.reshape(heads, 1), q, *([k_t] * pages_per_step), *([v_t] * pages_per_step))


def _pair_blockdiag(w):
    n, d, _ = w.shape
    w = w.reshape(n // 2, 2, d, d)
    z = jnp.zeros_like(w[:, 0])
    top = jnp.concatenate([w[:, 0], z], axis=2)
    bot = jnp.concatenate([z, w[:, 1]], axis=2)
    return jnp.concatenate([top, bot], axis=1)


def _state_to_pairs(s):
    b, h, n, _ = s.shape
    return _pair_blockdiag(s.reshape(b * h, n, n)).reshape(b, h // 2, 2 * n, 2 * n)


def _pairs_to_state(sp, n):
    b, hp = sp.shape[0], sp.shape[1]
    return jnp.stack([sp[:, :, :n, :n], sp[:, :, n:, n:]], axis=2).reshape(b, 2 * hp, n, n)


def _pad_rows(x, multiple):
    t = x.shape[1]
    t_pad = -(-t // multiple) * multiple
    return x if t_pad == t else jnp.pad(x, ((0, 0), (0, t_pad - t), (0, 0)))


def kernel(x_prompt, x_sample, state_rwkv_wkv, state_rwkv_shift, state_lru_h, state_lru_conv,
           cache_sb_k, cache_sb_v, page_table, cache_mem_k, cache_mem_v, mem_prompt,
           g_mix, g_mem, g_memkv, g_ffn, wq_mem, wk_mem, wv_mem, wo_mem, qn_mem, kn_mem,
           w_ffn_gate, w_ffn_up, w_ffn_down, w_in_ab, mu_shift, rw_w0, rw_w2, rw_a0, rw_a2,
           rw_g2, rw_kk, rw_ka, rw_rk, rw_lnx_w, rw_lnx_b, lru_conv_w, lru_conv_b, lru_wa,
           lru_ba, lru_wx, lru_bx, lru_lambda, w_out_ab, w_qkv_sb, w_out_sb, sb_bias):
    depth, d_model = g_mix.shape
    mem_heads = cache_mem_k.shape[3]
    mem_w = wq_mem.shape[2]
    rw_w = rw_w0.shape[1]
    rw_cols = mu_shift.shape[1]
    lru_w = lru_lambda.shape[1]
    rw_heads = rw_w // RW_HD
    sb_heads = sb_bias.shape[1]
    sb_w = sb_heads * SB_HD
    d_ff = w_ffn_gate.shape[2]
    lora_w = rw_w2.shape[1]
    lora_a = rw_a2.shape[1]
    assert lora_w + lora_a == LANES and rw_g2.shape[1] == LANES
    tf = 256
    assert d_ff % tf == 0
    bf = lambda z: z.astype(BF16)

    wg3 = [bf(w_ffn_gate[i]).reshape(d_model, d_ff // tf, tf).transpose(1, 0, 2) for i in range(depth)]
    wu3 = [bf(w_ffn_up[i]).reshape(d_model, d_ff // tf, tf).transpose(1, 0, 2) for i in range(depth)]
    wd3 = [bf(w_ffn_down[i]).reshape(d_ff // tf, tf, d_model) for i in range(depth)]
    head_id = jnp.arange(LANES) // RW_HD
    bd = (head_id[:, None] == head_id[None, :]).astype(BF16)

    def even_params(e):
        zw = jnp.zeros((lora_a, rw_w), F32)
        za = jnp.zeros((lora_w, rw_w), F32)
        row = lambda z: z.reshape(1, -1)
        rw = dict(mu=row(mu_shift[e]), w0=row(rw_w0[e]), w2=bf(jnp.concatenate([rw_w2[e], zw], axis=0)),
                  a0=row(rw_a0[e]), a2=bf(jnp.concatenate([za, rw_a2[e]], axis=0)), g2=bf(rw_g2[e]),
                  kk=row(rw_kk[e]), ka=row(rw_ka[e]), rk=row(rw_rk[e]), lnw=row(rw_lnx_w[e]),
                  lnb=row(rw_lnx_b[e]), bd=bd)
        lru = dict(cw=lru_conv_w[e], cb=row(lru_conv_b[e]), wa=bf(_pair_blockdiag(lru_wa[e])),
                   ba=row(lru_ba[e]), wx=bf(_pair_blockdiag(lru_wx[e])), bx=row(lru_bx[e]),
                   lam=row(lru_lambda[e]))
        return rw, lru

    def trunk(x, sample):
        bsz, t, _ = x.shape
        m = bsz * t
        tm = 512 if m % 512 == 0 else m
        out = {}
        xf = x.reshape(m, d_model)
        for i in range(depth):
            if i % 2 == 0:
                e = i // 2
                rw, lru = even_params(e)
                p_rw, p_x, p_gate = _norm_proj(xf, g_mix[i], bf(w_in_ab[e]),
                                               [rw_cols, rw_cols + lru_w], tm)
                p_rw = p_rw.reshape(bsz, t, rw_cols)
                p_x = p_x.reshape(bsz, t, lru_w)
                p_gate = p_gate.reshape(bsz, t, lru_w)
                if sample:
                    wkv0, shift0 = state_rwkv_wkv[e], state_rwkv_shift[e]
                    h0, conv0 = state_lru_h[e], state_lru_conv[e]
                else:
                    wkv0 = jnp.zeros((bsz, rw_heads, RW_HD, RW_HD), F32)
                    shift0 = jnp.zeros((bsz, rw_cols), F32)
                    h0 = jnp.zeros((bsz, lru_w), F32)
                    conv0 = jnp.zeros((bsz, CONV_W - 1, lru_w), F32)
                o_rw, s_pairs = _rwkv_mix(_pad_rows(p_rw, RW_CHUNK), shift0.reshape(bsz, 1, rw_cols),
                                          _state_to_pairs(wkv0), rw, RW_CHUNK, t)
                conv0_pad = jnp.pad(conv0, ((0, 0), (SUBLANES - (CONV_W - 1), 0), (0, 0)))
                px_pad = _pad_rows(p_x, BF16_ROWS)
                c_lru = LRU_CHUNK if px_pad.shape[1] % LRU_CHUNK == 0 else px_pad.shape[1]
                o_lru, h_last = _lru_mix(px_pad, _pad_rows(p_gate, BF16_ROWS), conv0_pad,
                                         h0.reshape(bsz, 1, lru_w), lru, c_lru, t)
                out.setdefault("wkv", []).append(_pairs_to_state(s_pairs, RW_HD))
                out.setdefault("shift", []).append(p_rw[:, t - 1])
                out.setdefault("lru_h", []).append(h_last[:, 0])
                out.setdefault("lru_conv", []).append(
                    jnp.concatenate([conv0, p_x], axis=1)[:, -(CONV_W - 1):])
                w_out = bf(w_out_ab[e])
                mix_in, mix_w = [o_rw[:, :t], o_lru[:, :t]], [w_out[:rw_w], w_out[rw_w:]]
            else:
                o = i // 2
                q, k, v = (z.reshape(bsz, t, sb_w) for z in
                           _norm_proj(xf, g_mix[i], bf(w_qkv_sb[o]), [sb_w, 2 * sb_w], tm))
                if sample:
                    pool = cache_sb_k.shape[1]
                    k_t = jnp.transpose(cache_sb_k, (0, 1, 3, 4, 2)).reshape(-1, PAGE)
                    v_t = jnp.transpose(cache_sb_v, (0, 1, 3, 4, 2)).reshape(-1, PAGE)
                    att = _sb_decode(q, k_t, v_t, page_table + o * pool, sb_bias[o],
                                     sb_heads, DECODE_PAGES_PER_STEP)
                else:
                    att = _sb_prompt(q, k, v, sb_bias[o], sb_heads)
                out.setdefault("sb_k", []).append(k.reshape(bsz, t, sb_heads, SB_HD))
                out.setdefault("sb_v", []).append(v.reshape(bsz, t, sb_heads, SB_HD))
                mix_in, mix_w = [att.reshape(bsz, t, sb_w)], [bf(w_out_sb[o])]
            if sample:
                mk = cache_mem_k[i].reshape(bsz, -1, mem_w)
                mv = cache_mem_v[i].reshape(bsz, -1, mem_w)
            else:
                mk, mv = _memory_kv(mem_prompt, g_memkv[i], bf(wk_mem[i]), bf(wv_mem[i]), kn_mem[i],
                                    mem_heads)
                out.setdefault("mem_k", []).append(mk.reshape(bsz, -1, mem_heads, mem_w // mem_heads))
                out.setdefault("mem_v", []).append(mv.reshape(bsz, -1, mem_heads, mem_w // mem_heads))
            x3 = _pad_rows(xf.reshape(bsz, t, d_model), BF16_ROWS)
            t3 = x3.shape[1]
            big = t3 % 512 == 0
            x3 = _memx_residual(x3, [_pad_rows(a, BF16_ROWS) for a in mix_in], mix_w, g_mem[i],
                                bf(wq_mem[i]), qn_mem[i], mk, mv, bf(wo_mem[i]), mem_heads,
                                512 if big else t3, 1 if big or bsz % MEMX_ROWS_PER_STEP else MEMX_ROWS_PER_STEP)
            xf = x3[:, :t].reshape(m, d_model)
            xf = _ffn_residual(xf, g_ffn[i], wg3[i], wu3[i], wd3[i], tm)
        return xf.reshape(bsz, t, d_model), out

    y_p, sp = trunk(x_prompt, False)
    y_s, ss = trunk(x_sample, True)
    st = lambda arrs: jnp.stack(arrs, axis=0)
    return (y_p, y_s, st(sp["wkv"]), st(ss["wkv"]), st(sp["shift"]), st(ss["shift"]),
            st(sp["lru_h"]), st(ss["lru_h"]), st(sp["lru_conv"]), st(ss["lru_conv"]),
            st(sp["sb_k"]), st(ss["sb_k"]), st(sp["sb_v"]), st(ss["sb_v"]),
            st(sp["mem_k"]), st(sp["mem_v"]))
```

```python
import functools
import math

import jax
import jax.numpy as jnp
from jax import lax
from jax.experimental import pallas as pl
from jax.experimental.pallas import tpu as pltpu

F32 = jnp.float32
BF16 = jnp.bfloat16

NORM_EPS = 1e-6
RW_HD = 64
LANES = 128
SUBLANES = 8
BF16_ROWS = 16
RW_CHUNK = 64
RW_ROWS_PER_STEP = 4
SB_PAIRS_PER_STEP = 4
MEMX_ROWS_PER_STEP = 8
DECODE_PAGES_PER_STEP = 8
LRU_CHUNK = 256
RW_LN_EPS = RW_HD * 1e-5
LRU_C = 8.0
LRU_BD = 64
CONV_W = 4
SB_HD = 64
PAGE = 128
SB_TILE = 256
EXP_M05 = math.exp(-0.5)
VMEM_BYTES_V7X = 64 * 1024 * 1024


def _cparams(sem, vmem_mb):
    assert vmem_mb * 1024 * 1024 < VMEM_BYTES_V7X
    return pltpu.CompilerParams(dimension_semantics=sem, vmem_limit_bytes=vmem_mb * 1024 * 1024)


def _dot(a, b):
    return jnp.dot(a, b, preferred_element_type=F32)


def _dot_nt(a, b):
    return lax.dot_general(a, b, (((1,), (1,)), ((), ())), preferred_element_type=F32)


def _dot_tn(a, b):
    return lax.dot_general(a, b, (((0,), (0,)), ((), ())), preferred_element_type=F32)


def _split2(x):
    hi = x.astype(BF16)
    lo = (x - hi.astype(F32)).astype(BF16)
    return hi, lo


def _split3(x):
    hi = x.astype(BF16)
    r = x - hi.astype(F32)
    mid = r.astype(BF16)
    lo = (r - mid.astype(F32)).astype(BF16)
    return hi, mid, lo


def _dot_xw(x, w_bf16, parts=2):
    pieces = _split2(x) if parts == 2 else _split3(x)
    out = _dot(pieces[0], w_bf16)
    for p in pieces[1:]:
        out = out + _dot(p, w_bf16)
    return out


def _dot_wx(w_bf16, x, parts=3):
    pieces = _split2(x) if parts == 2 else _split3(x)
    out = _dot(w_bf16, pieces[0])
    for p in pieces[1:]:
        out = out + _dot(w_bf16, p)
    return out


def _mm3(a, b, dot):
    ah, al = _split2(a)
    bh, bl = _split2(b)
    return dot(ah, bh) + (dot(al, bh) + dot(ah, bl))


def _mm1(a, b, dot):
    return dot(a.astype(BF16), b.astype(BF16))


def _softplus(x):
    return jnp.maximum(x, 0.0) + jnp.log1p(jnp.exp(-jnp.abs(x)))


def _gelu_tanh(x):
    return 0.5 * x * (1.0 + jnp.tanh(math.sqrt(2.0 / math.pi) * (x + 0.044715 * (x * x * x))))


def _rms(x, g, eps=NORM_EPS):
    ms = jnp.mean(x * x, axis=-1, keepdims=True)
    return x * lax.rsqrt(ms + eps) * g


def _norm_proj_kernel(x_ref, g_ref, w_ref, *o_refs, splits, chunk):
    h = _rms(x_ref[...], g_ref[...]).astype(BF16)
    for o_ref, (s, e) in zip(o_refs, splits):
        for c0 in range(s, e, chunk):
            c1 = min(c0 + chunk, e)
            o_ref[:, c0 - s:c1 - s] = _dot(h, w_ref[:, c0:c1])


def _norm_proj(x, g, w_bf16, splits, tm):
    m, d = x.shape
    n = w_bf16.shape[1]
    assert m % tm == 0
    bounds = [0] + list(splits) + [n]
    ranges = [(bounds[i], bounds[i + 1]) for i in range(len(bounds) - 1)]
    out_shape = [jax.ShapeDtypeStruct((m, e - s), F32) for s, e in ranges]
    out_specs = [pl.BlockSpec((tm, e - s), lambda i: (i, 0)) for s, e in ranges]
    return pl.pallas_call(
        functools.partial(_norm_proj_kernel, splits=ranges, chunk=512),
        grid=(m // tm,),
        in_specs=[pl.BlockSpec((tm, d), lambda i: (i, 0)),
                  pl.BlockSpec((1, d), lambda i: (0, 0)),
                  pl.BlockSpec((d, n), lambda i: (0, 0))],
        out_specs=out_specs,
        out_shape=out_shape,
        compiler_params=_cparams(("parallel",), 48),
        name="norm_proj",
    )(x, g.reshape(1, d), w_bf16)


def _ffn_kernel(x_ref, g_ref, wg_ref, wu_ref, wd_ref, o_ref, h_ref):
    x = x_ref[...]
    h_ref[...] = _rms(x, g_ref[...]).astype(BF16)
    o_ref[...] = x

    def body(j, carry):
        h = h_ref[...]
        gate = _dot(h, wg_ref[j])
        up = _dot(h, wu_ref[j])
        act = (gate * jax.nn.sigmoid(gate) * up).astype(BF16)
        o_ref[...] += _dot(act, wd_ref[j])
        return carry

    lax.fori_loop(0, wg_ref.shape[0], body, 0)


def _ffn_residual(x, g, wg3, wu3, wd3, tm):
    m, d = x.shape
    nf, _, tf = wg3.shape
    assert m % tm == 0
    return pl.pallas_call(
        _ffn_kernel,
        grid=(m // tm,),
        in_specs=[pl.BlockSpec((tm, d), lambda i: (i, 0)),
                  pl.BlockSpec((1, d), lambda i: (0, 0)),
                  pl.BlockSpec((nf, d, tf), lambda i: (0, 0, 0)),
                  pl.BlockSpec((nf, d, tf), lambda i: (0, 0, 0)),
                  pl.BlockSpec((nf, tf, d), lambda i: (0, 0, 0))],
        out_specs=pl.BlockSpec((tm, d), lambda i: (i, 0)),
        out_shape=jax.ShapeDtypeStruct((m, d), F32),
        scratch_shapes=[pltpu.VMEM((tm, d), BF16)],
        compiler_params=_cparams(("parallel",), 56),
        name="ffn_residual",
    )(x, g.reshape(1, d), wg3, wu3, wd3)


def _memkv_kernel(m_ref, g_ref, wk_ref, wv_ref, kn_ref, k_ref, v_ref, *, heads, hd):
    mn = _rms(m_ref[0], g_ref[...]).astype(BF16)
    k = _dot(mn, wk_ref[...])
    v_ref[0] = _dot(mn, wv_ref[...])
    for hh in range(heads):
        sl = slice(hh * hd, (hh + 1) * hd)
        k_ref[0, :, sl] = _rms(k[:, sl], kn_ref[...])


def _memory_kv(mem, g, wk, wv, kn, heads):
    b, nm, d = mem.shape
    w = wk.shape[1]
    hd = w // heads
    return pl.pallas_call(
        functools.partial(_memkv_kernel, heads=heads, hd=hd),
        grid=(b,),
        in_specs=[pl.BlockSpec((1, nm, d), lambda i: (i, 0, 0)),
                  pl.BlockSpec((1, d), lambda i: (0, 0)),
                  pl.BlockSpec((d, w), lambda i: (0, 0)),
                  pl.BlockSpec((d, w), lambda i: (0, 0)),
                  pl.BlockSpec((1, hd), lambda i: (0, 0))],
        out_specs=[pl.BlockSpec((1, nm, w), lambda i: (i, 0, 0)),
                   pl.BlockSpec((1, nm, w), lambda i: (i, 0, 0))],
        out_shape=[jax.ShapeDtypeStruct((b, nm, w), F32)] * 2,
        compiler_params=_cparams(("parallel",), 32),
        name="memory_kv",
    )(mem, g.reshape(1, d), wk, wv, kn.reshape(1, hd))


def _memx_kernel(*refs, n_proj, heads, hd):
    x_ref, a_refs, w_refs = refs[0], refs[1:1 + n_proj], refs[1 + n_proj:1 + 2 * n_proj]
    g_ref, wq_ref, qn_ref, k_ref, v_ref, wo_ref, o_ref = refs[1 + 2 * n_proj:]
    nb = x_ref.shape[0]
    scale = hd ** -0.5
    x1 = []
    for bi in range(nb):
        x = x_ref[bi]
        for a_ref, w_ref in zip(a_refs, w_refs):
            x = x + _dot(a_ref[bi].astype(BF16), w_ref[...])
        x1.append(x)
    qs = [_dot(_rms(x, g_ref[...]).astype(BF16), wq_ref[...]) for x in x1]
    chains = [(bi, slice(hh * hd, (hh + 1) * hd)) for bi in range(nb) for hh in range(heads)]
    qh = [_rms(qs[bi][:, sl], qn_ref[...]).astype(BF16) for bi, sl in chains]
    s = [_dot_nt(qh[c], k_ref[bi, :, sl].astype(BF16)) * scale for c, (bi, sl) in enumerate(chains)]
    e = [jnp.exp(z - jnp.max(z, axis=-1, keepdims=True)) for z in s]
    p = [(z / jnp.sum(z, axis=-1, keepdims=True)).astype(BF16) for z in e]
    oh = [_dot(p[c], v_ref[bi, :, sl].astype(BF16)) for c, (bi, sl) in enumerate(chains)]
    for bi in range(nb):
        o = jnp.concatenate(oh[bi * heads:(bi + 1) * heads], axis=1).astype(BF16)
        o_ref[bi] = x1[bi] + _dot(o, wo_ref[...])


def _memx_residual(x, a_list, w_list, g, wq, qn, k, v, wo, heads, tm, nb):
    b, t, d = x.shape
    nm, w = k.shape[1], k.shape[2]
    hd = w // heads
    assert t % tm == 0 and b % nb == 0
    rows = lambda n: pl.BlockSpec((nb, tm, n), lambda i, j: (i, j, 0))
    full = lambda arr: pl.BlockSpec(arr.shape, lambda i, j: (0,) * arr.ndim)
    mem = pl.BlockSpec((nb, nm, w), lambda i, j: (i, 0, 0))
    g2, qn2 = g.reshape(1, d), qn.reshape(1, hd)
    return pl.pallas_call(
        functools.partial(_memx_kernel, n_proj=len(a_list), heads=heads, hd=hd),
        grid=(b // nb, t // tm),
        in_specs=[rows(d)] + [rows(a.shape[2]) for a in a_list] + [full(wi) for wi in w_list]
                 + [full(g2), full(wq), full(qn2), mem, mem, full(wo)],
        out_specs=rows(d),
        out_shape=jax.ShapeDtypeStruct((b, t, d), F32),
        compiler_params=_cparams(("parallel", "parallel"), 40),
        name="memx_residual",
    )(x, *a_list, *w_list, g2, wq, qn2, k, v, wo)


def _unit_lower_inverse(lmats, n, block):
    row = lax.broadcasted_iota(jnp.int32, (n, n), 0)
    col = lax.broadcasted_iota(jnp.int32, (n, n), 1)
    eye = jnp.where(row == col, 1.0, 0.0).astype(F32)
    first = ((row ^ col) < 2) & ((row & 1) != 0) & ((col & 1) == 0)
    ts = [eye - jnp.where(first, l, 0.0) for l in lmats]
    s = 2
    while s < block:
        lower_left = ((row ^ col) < 2 * s) & ((row & s) != 0) & ((col & s) == 0)
        us = [_mm1(jnp.where(lower_left, l, 0.0), t, _dot) for l, t in zip(lmats, ts)]
        ts = [t - _mm1(t, u, _dot) for t, u in zip(ts, us)]
        s *= 2
    return ts


def _rwkv_kernel(p_ref, sh0_ref, s0_ref, mu_ref, w0_ref, w2_ref, a0_ref, a2_ref, g2_ref,
                 kk_ref, ka_ref, rk_ref, lw_ref, lb_ref, bd_ref,
                 o_ref, s_out_ref, s_ref, prev_ref, *, c, t_valid, width):
    ci = pl.program_id(1)
    nb = p_ref.shape[0]
    npairs = width // LANES
    sls = [slice(pi * LANES, (pi + 1) * LANES) for pi in range(npairs)]

    @pl.when(ci == 0)
    def _():
        s_ref[...] = s0_ref[...]
        prev_ref[:, 0:1, :] = sh0_ref[...]

    bd = bd_ref[...]

    def head_sum(z):
        rows = jnp.concatenate([z[:, sl] for sl in sls], axis=0)
        tot = _dot_xw(rows, bd)
        return jnp.concatenate([tot[i * c:(i + 1) * c] for i in range(npairs)], axis=1)

    row1 = lax.broadcasted_iota(jnp.int32, (c, 1), 0)
    rr = lax.broadcasted_iota(jnp.int32, (c, c), 0)
    cc = lax.broadcasted_iota(jnp.int32, (c, c), 1)
    tril = jnp.where(cc <= rr, 1.0, 0.0).astype(BF16)

    def prepare(bi):
        p = p_ref[bi]
        prev = jnp.where(row1 == 0, prev_ref[bi, 0:1, :], pltpu.roll(p, 1, axis=0))
        prev_ref[bi, 0:1, :] = p[c - 1:c, :]
        xs = p + mu_ref[...] * (prev - p)
        r = xs[:, 0:width]
        k = xs[:, width:2 * width]
        v = xs[:, 2 * width:3 * width]
        dwa = xs[:, 3 * width:3 * width + LANES]
        dg = xs[:, 3 * width + LANES:3 * width + 2 * LANES]
        u = w0_ref[...] + _dot(jnp.tanh(dwa).astype(BF16), w2_ref[...])
        logw = -EXP_M05 * jax.nn.sigmoid(u)
        a = jax.nn.sigmoid(a0_ref[...] + _dot(dwa.astype(BF16), a2_ref[...]))
        g = _dot(jax.nn.sigmoid(dg).astype(BF16), g2_ref[...])
        kkr = k * kk_ref[...]
        kk = kkr / jnp.maximum(jnp.sqrt(head_sum(kkr * kkr)), 1e-12)
        kmod = k * (1.0 + (a - 1.0) * ka_ref[...])
        if t_valid % c != 0:
            valid = ((ci * c + row1) < t_valid).astype(F32)
            logw, kk, kmod, v = logw * valid, kk * valid, kmod * valid, v * valid
        b = kk * a
        lc = _dot_wx(tril, logw, parts=3)
        lc_end = lc[c - 1:c, :]
        inv_c = jnp.exp(-lc)
        to_end = jnp.exp(lc_end - lc)
        return dict(r=r, v=v, kmod=kmod, g=g, kk_t=kk * jnp.exp(lc - logw), b_t=b * inv_c,
                    k_t=kmod * inv_c, r_t=r * jnp.exp(lc), b_end=b * to_end, k_end=kmod * to_end,
                    c_end=jnp.exp(lc_end))

    preps = [prepare(bi) for bi in range(nb)]

    n2 = 2 * c
    lane = lax.broadcasted_iota(jnp.int32, (1, LANES), 1)
    m0 = (lane < RW_HD).astype(F32)
    m1 = 1.0 - m0
    r2 = lax.broadcasted_iota(jnp.int32, (n2, n2), 0)
    c2 = lax.broadcasted_iota(jnp.int32, (n2, n2), 1)
    strict = c2 < r2
    incl = c2 <= r2

    def stack(z):
        return jnp.concatenate([z * m0, z * m1], axis=0).astype(BF16)

    chains = [(bi, pi) for bi in range(nb) for pi in range(npairs)]
    ids = range(len(chains))
    stk = lambda name: [stack(preps[bi][name][:, sls[pi]]) for bi, pi in chains]
    kks, bs, ks, rs, vs = stk("kk_t"), stk("b_t"), stk("k_t"), stk("r_t"), stk("v")
    s_old = [s_ref[bi, pi] for bi, pi in chains]
    s_bf = [s.astype(BF16) for s in s_old]
    a_kb = [jnp.where(strict, _dot_nt(kks[i], bs[i]), 0.0) for i in ids]
    a_kk = [jnp.where(strict, _dot_nt(kks[i], ks[i]), 0.0).astype(BF16) for i in ids]
    rhs = [-(_dot_nt(kks[i], s_bf[i]) + _dot(a_kk[i], vs[i])) for i in ids]
    tinv = _unit_lower_inverse(a_kb, n2, c)
    sas = [_mm1(tinv[i], rhs[i], _dot).astype(BF16) for i in ids]
    a_rb = [jnp.where(incl, _dot_nt(rs[i], bs[i]), 0.0).astype(BF16) for i in ids]
    a_rk = [jnp.where(incl, _dot_nt(rs[i], ks[i]), 0.0).astype(BF16) for i in ids]
    y2 = [_dot_nt(rs[i], s_bf[i]) + _dot(a_rb[i], sas[i]) + _dot(a_rk[i], vs[i]) for i in ids]
    for i, (bi, pi) in enumerate(chains):
        s_ref[bi, pi] = (s_old[i] * preps[bi]["c_end"][:, sls[pi]]
                         + _dot_tn(sas[i], stack(preps[bi]["b_end"][:, sls[pi]]))
                         + _dot_tn(vs[i], stack(preps[bi]["k_end"][:, sls[pi]])))

    inv_n = 1.0 / RW_HD
    for bi in range(nb):
        pr = preps[bi]
        y = jnp.concatenate([y2[bi * npairs + pi][0:c] + y2[bi * npairs + pi][c:n2]
                             for pi in range(npairs)], axis=1)
        mean = head_sum(y) * inv_n
        d = y - mean
        var = head_sum(d * d) * inv_n
        yn = d * lax.rsqrt(var + RW_LN_EPS) * lw_ref[...] + lb_ref[...]
        bonus = head_sum(pr["r"] * pr["kmod"] * rk_ref[...]) * pr["v"]
        o_ref[bi] = (yn + bonus) * pr["g"]

    @pl.when(ci == pl.num_programs(1) - 1)
    def _():
        s_out_ref[...] = s_ref[...]


def _rwkv_mix(p_rw, shift0, s0_pairs, prm, c, t_valid):
    b, tp, cols = p_rw.shape
    width = prm["w0"].shape[1]
    npairs = width // LANES
    nb = RW_ROWS_PER_STEP if b % RW_ROWS_PER_STEP == 0 else 1
    assert tp % c == 0 and cols == 3 * width + 2 * LANES
    vec = lambda n: pl.BlockSpec((1, n), lambda i, j: (0, 0))
    mat = lambda r, n: pl.BlockSpec((r, n), lambda i, j: (0, 0))
    return pl.pallas_call(
        functools.partial(_rwkv_kernel, c=c, t_valid=t_valid, width=width),
        grid=(b // nb, tp // c),
        in_specs=[pl.BlockSpec((nb, c, cols), lambda i, j: (i, j, 0)),
                  pl.BlockSpec((nb, 1, cols), lambda i, j: (i, 0, 0)),
                  pl.BlockSpec((nb, npairs, LANES, LANES), lambda i, j: (i, 0, 0, 0)),
                  vec(cols), vec(width), mat(LANES, width), vec(width), mat(LANES, width),
                  mat(LANES, width), vec(width), vec(width), vec(width), vec(width), vec(width),
                  mat(LANES, LANES)],
        out_specs=[pl.BlockSpec((nb, c, width), lambda i, j: (i, j, 0)),
                   pl.BlockSpec((nb, npairs, LANES, LANES), lambda i, j: (i, 0, 0, 0))],
        out_shape=[jax.ShapeDtypeStruct((b, tp, width), F32),
                   jax.ShapeDtypeStruct((b, npairs, LANES, LANES), F32)],
        scratch_shapes=[pltpu.VMEM((nb, npairs, LANES, LANES), F32),
                        pltpu.VMEM((nb, SUBLANES, cols), F32)],
        compiler_params=_cparams(("parallel", "arbitrary"), 32),
        name="rwkv_mix",
    )(p_rw, shift0, s0_pairs, prm["mu"], prm["w0"], prm["w2"], prm["a0"], prm["a2"], prm["g2"],
      prm["kk"], prm["ka"], prm["rk"], prm["lnw"], prm["lnb"], prm["bd"])


def _lru_kernel(px_ref, pg_ref, conv0_ref, h0_ref, cw_ref, cb_ref, wa_ref, ba_ref, wx_ref, bx_ref,
                lam_ref, o_ref, hl_ref, ext_ref, h_ref, *, c, t_valid, width):
    ci = pl.program_id(1)
    npairs = width // LANES

    @pl.when(ci == 0)
    def _():
        ext_ref[0:SUBLANES, :] = conv0_ref[0]
        h_ref[0:1, :] = h0_ref[0]

    px = px_ref[0]
    ext_ref[SUBLANES:SUBLANES + c, :] = px
    acc = None
    for j in range(CONV_W - 1):
        term = cw_ref[j:j + 1, :] * ext_ref[SUBLANES - (CONV_W - 1) + j:SUBLANES - (CONV_W - 1) + j + c, :]
        acc = term if acc is None else acc + term
    xc = cb_ref[...] + (acc + cw_ref[CONV_W - 1:CONV_W, :] * px)
    ext_ref[0:SUBLANES, :] = ext_ref[c:c + SUBLANES, :]

    xcb = xc.astype(BF16)
    ga, gx = [], []
    for pi in range(npairs):
        sl = slice(pi * LANES, (pi + 1) * LANES)
        ga.append(_dot(xcb[:, sl], wa_ref[pi]))
        gx.append(_dot(xcb[:, sl], wx_ref[pi]))
    gate_r = jax.nn.sigmoid(jnp.concatenate(ga, axis=1) + ba_ref[...])
    gate_i = jax.nn.sigmoid(jnp.concatenate(gx, axis=1) + bx_ref[...])
    log_a = (-LRU_C * gate_r) * _softplus(-lam_ref[...])
    a_t = jnp.exp(log_a)
    b_t = jnp.sqrt(-jnp.tanh(log_a) * (a_t * a_t + 1.0)) * (gate_i * xc)

    row = lax.broadcasted_iota(jnp.int32, (c, 1), 0)
    b_t = b_t + jnp.where(row == 0, a_t * h_ref[0:1, :], 0.0)
    d = 1
    while d < c:
        keep = row >= d
        a_sh = jnp.where(keep, pltpu.roll(a_t, d, axis=0), 1.0)
        b_sh = jnp.where(keep, pltpu.roll(b_t, d, axis=0), 0.0)
        b_t = a_t * b_sh + b_t
        a_t = a_t * a_sh
        d *= 2
    hs = b_t
    h_ref[0:1, :] = hs[c - 1:c, :]
    o_ref[0] = hs * _gelu_tanh(pg_ref[0])

    last = t_valid - 1

    @pl.when(ci == last // c)
    def _():
        hl_ref[0] = hs[last % c:last % c + 1, :]


def _lru_mix(p_x, p_gate, conv0_pad, h0, prm, c, t_valid):
    b, tp, width = p_x.shape
    npairs = width // LANES
    assert tp % c == 0 and c >= SUBLANES
    vec = lambda n: pl.BlockSpec((1, n), lambda i, j: (0, 0))
    blk = pl.BlockSpec((npairs, LANES, LANES), lambda i, j: (0, 0, 0))
    return pl.pallas_call(
        functools.partial(_lru_kernel, c=c, t_valid=t_valid, width=width),
        grid=(b, tp // c),
        in_specs=[pl.BlockSpec((1, c, width), lambda i, j: (i, j, 0)),
                  pl.BlockSpec((1, c, width), lambda i, j: (i, j, 0)),
                  pl.BlockSpec((1, SUBLANES, width), lambda i, j: (i, 0, 0)),
                  pl.BlockSpec((1, 1, width), lambda i, j: (i, 0, 0)),
                  pl.BlockSpec((CONV_W, width), lambda i, j: (0, 0)),
                  vec(width), blk, vec(width), blk, vec(width), vec(width)],
        out_specs=[pl.BlockSpec((1, c, width), lambda i, j: (i, j, 0)),
                   pl.BlockSpec((1, 1, width), lambda i, j: (i, 0, 0))],
        out_shape=[jax.ShapeDtypeStruct((b, tp, width), F32),
                   jax.ShapeDtypeStruct((b, 1, width), F32)],
        scratch_shapes=[pltpu.VMEM((c + SUBLANES, width), F32),
                        pltpu.VMEM((SUBLANES, width), F32)],
        compiler_params=_cparams(("parallel", "arbitrary"), 32),
        name="lru_mix",
    )(p_x, p_gate, conv0_pad, h0, prm["cw"], prm["cb"], prm["wa"], prm["ba"], prm["wx"], prm["bx"],
      prm["lam"])


def _sb_weights(zs, cums, upper2, masks):
    sps = [jnp.maximum(z, 0.0) + jnp.log(1.0 + jnp.exp(-jnp.abs(z))) for z in zs]
    sps = [sp if m is None else jnp.where(m, sp, 0.0) for sp, m in zip(sps, masks)]
    suffix = [_dot(jnp.concatenate(_split2(sp), axis=1), upper2) for sp in sps]
    atts = [jnp.exp((z - c) - sf) for z, c, sf in zip(zs, cums, suffix)]
    atts = [a if m is None else jnp.where(m, a, 0.0) for a, m in zip(atts, masks)]
    cums = [c + sf[:, 0:1] for c, sf in zip(cums, suffix)]
    return atts, cums


def _upper2(n):
    jj = lax.broadcasted_iota(jnp.int32, (2 * n, n), 0) & (n - 1)
    ss = lax.broadcasted_iota(jnp.int32, (2 * n, n), 1)
    return jnp.where(jj >= ss, 1.0, 0.0).astype(BF16)


def _sb_prompt_kernel(bias_ref, q_ref, k_ref, v_ref, o_ref, z_ref, att_ref, acc_ref, cum_ref):
    pi = pl.program_id(1)
    qi = pl.program_id(2)
    blk = SB_TILE
    npairs = q_ref.shape[2] // LANES
    nheads = 2 * npairs
    heads = range(nheads)
    sls = [slice(p * LANES, (p + 1) * LANES) for p in range(npairs)]
    lane = lax.broadcasted_iota(jnp.int32, (1, LANES), 1)
    first = lane < SB_HD
    q = q_ref[0] * (SB_HD ** -0.5)
    qs = [jnp.where(first == (h % 2 == 0), q[:, sls[h // 2]], 0.0).astype(BF16) for h in heads]
    biases = [bias_ref[nheads * pi + h] for h in heads]
    upper2 = _upper2(blk)

    def scores(kb):
        start = pl.multiple_of(kb * blk, blk)
        kblk = [k_ref[0, pl.ds(start, blk), sl].astype(BF16) for sl in sls]
        return [_dot_nt(qs[h], kblk[h // 2]) + biases[h] for h in heads]

    def weighted_values(atts, kb):
        start = pl.multiple_of(kb * blk, blk)
        vblk = [v_ref[0, pl.ds(start, blk), sl].astype(BF16) for sl in sls]
        return [_dot(atts[h], vblk[h // 2]) for h in heads]

    qpos = lax.broadcasted_iota(jnp.int32, (blk, blk), 0)
    kpos = lax.broadcasted_iota(jnp.int32, (blk, blk), 1)
    mask = kpos < qpos
    zero = jnp.zeros((blk, 1), F32)
    atts, cums = _sb_weights(scores(qi), [zero] * nheads, upper2, [mask] * nheads)
    zs = scores(jnp.maximum(qi - 1, 0))
    for h in heads:
        att_ref[0, h] = atts[h].astype(BF16)
        z_ref[0, h] = zs[h]
        acc_ref[h] = jnp.zeros((blk, LANES), F32)
        cum_ref[h] = cums[h]

    def trip(kb, cur, nxt):
        zs_next = scores(jnp.maximum(kb - 1, 0))
        for h in heads:
            z_ref[nxt, h] = zs_next[h]
        pvs = weighted_values([att_ref[cur, h] for h in heads], kb + 1)
        for h in heads:
            acc_ref[h] += pvs[h]
        atts, cums = _sb_weights([z_ref[cur, h] for h in heads], [cum_ref[h] for h in heads],
                                 upper2, [None] * nheads)
        for h in heads:
            att_ref[nxt, h] = atts[h].astype(BF16)
            cum_ref[h] = cums[h]

    def body(j, carry):
        kb = qi - 1 - 2 * j
        trip(kb, 0, 1)

        @pl.when(kb >= 1)
        def _():
            trip(kb - 1, 1, 0)

        return carry

    lax.fori_loop(0, (qi + 1) // 2, body, 0)
    last = qi & 1
    pvs = weighted_values([att_ref[last, h] for h in heads], 0)
    outs = [acc_ref[h] + pvs[h] for h in heads]
    o_ref[0] = jnp.concatenate([jnp.where(first, outs[2 * p], outs[2 * p + 1]) for p in range(npairs)],
                               axis=1)


def _sb_prompt(qkv, bias, heads):
    b, t, w3 = qkv.shape
    w = w3 // 3
    npairs = w // LANES
    pp = SB_PAIRS_PER_STEP if npairs % SB_PAIRS_PER_STEP == 0 else 1
    ngroups = npairs // pp
    gw = pp * LANES
    nh = 2 * pp
    assert t % SB_TILE == 0 and heads * SB_HD == w
    return pl.pallas_call(
        _sb_prompt_kernel,
        grid=(b, ngroups, t // SB_TILE),
        in_specs=[pl.BlockSpec(memory_space=pltpu.SMEM),
                  pl.BlockSpec((1, SB_TILE, gw), lambda i, p, j: (i, j, p)),
                  pl.BlockSpec((1, t, gw), lambda i, p, j: (i, 0, ngroups + p)),
                  pl.BlockSpec((1, t, gw), lambda i, p, j: (i, 0, 2 * ngroups + p))],
        out_specs=pl.BlockSpec((1, SB_TILE, gw), lambda i, p, j: (i, j, p)),
        out_shape=jax.ShapeDtypeStruct((b, t, w), F32),
        scratch_shapes=[pltpu.VMEM((2, nh, SB_TILE, SB_TILE), F32),
                        pltpu.VMEM((2, nh, SB_TILE, SB_TILE), BF16),
                        pltpu.VMEM((nh, SB_TILE, LANES), F32),
                        pltpu.VMEM((nh, SB_TILE, 1), F32)],
        compiler_params=_cparams(("parallel", "parallel", "arbitrary"), 40),
        name="sb_prompt",
    )(bias, qkv, qkv, qkv)


def _sb_decode_kernel(pt_ref, bias_ref, q_ref, *refs, pages_per_step, heads):
    k_refs = refs[:pages_per_step]
    v_refs = refs[pages_per_step:2 * pages_per_step]
    o_ref, acc_ref, cum_ref = refs[2 * pages_per_step:]
    j = pl.program_id(1)
    w = heads * SB_HD

    @pl.when(j == 0)
    def _():
        acc_ref[...] = jnp.zeros_like(acc_ref)
        cum_ref[...] = jnp.zeros_like(cum_ref)

    hrow = lax.broadcasted_iota(jnp.int32, (heads, w), 0)
    hcol = lax.broadcasted_iota(jnp.int32, (heads, w), 1) >> (SB_HD.bit_length() - 1)
    own = hrow == hcol
    qm = jnp.where(own, q_ref[0] * (SB_HD ** -0.5), 0.0).astype(BF16)
    upper2 = _upper2(PAGE)
    bias = bias_ref[...]

    pages = range(pages_per_step)
    zs = [_dot(qm, k_refs[i][...].astype(BF16)) + bias for i in pages]
    none = [None] * pages_per_step
    zero = jnp.zeros((heads, 1), F32)
    atts, tots = _sb_weights(zs, [zero] * pages_per_step, upper2, none)
    acc = acc_ref[...]
    cum = cum_ref[...]
    for i in pages:
        att = atts[i] * jnp.exp(-cum)
        acc = acc + _dot_nt(att.astype(BF16), v_refs[i][...].astype(BF16))
        cum = cum + tots[i]
    acc_ref[...] = acc
    cum_ref[...] = cum

    @pl.when(j == pl.num_programs(1) - 1)
    def _():
        o_ref[0] = jnp.sum(jnp.where(own, acc, 0.0), axis=0, keepdims=True)


def _sb_decode(q, k_t, v_t, page_table, bias, heads, pages_per_step):
    b, _, w = q.shape
    n_pages = page_table.shape[1]
    assert n_pages % pages_per_step == 0 and w == heads * SB_HD and k_t.shape[1] == PAGE
    steps = n_pages // pages_per_step

    def page_spec(i):
        return pl.BlockSpec((w, PAGE),
                            lambda bi, j, pt: (pt[bi, n_pages - 1 - (j * pages_per_step + i)], 0))

    return pl.pallas_call(
        functools.partial(_sb_decode_kernel, pages_per_step=pages_per_step, heads=heads),
        grid_spec=pltpu.PrefetchScalarGridSpec(
            num_scalar_prefetch=1,
            grid=(b, steps),
            in_specs=[pl.BlockSpec((heads, 1), lambda bi, j, pt: (0, 0)),
                      pl.BlockSpec((1, 1, w), lambda bi, j, pt: (bi, 0, 0))]
                     + [page_spec(i) for i in range(pages_per_step)] * 2,
            out_specs=pl.BlockSpec((1, 1, w), lambda bi, j, pt: (bi, 0, 0)),
            scratch_shapes=[pltpu.VMEM((heads, w), F32), pltpu.VMEM((heads, 1), F32)]),
        out_shape=jax.ShapeDtypeStruct((b, 1, w), F32),
        compiler_params=_cparams(("parallel", "arbitrary"), 48),
        name="sb_decode",
    )(page_table, bias.reshape(heads, 1), q, *([k_t] * pages_per_step), *([v_t] * pages_per_step))


def _pair_blockdiag(w):
    n, d, _ = w.shape
    w = w.reshape(n // 2, 2, d, d)
    z = jnp.zeros_like(w[:, 0])
    top = jnp.concatenate([w[:, 0], z], axis=2)
    bot = jnp.concatenate([z, w[:, 1]], axis=2)
    return jnp.concatenate([top, bot], axis=1)


def _state_to_pairs(s):
    b, h, n, _ = s.shape
    return _pair_blockdiag(s.reshape(b * h, n, n)).reshape(b, h // 2, 2 * n, 2 * n)


def _pairs_to_state(sp, n):
    b, hp = sp.shape[0], sp.shape[1]
    return jnp.stack([sp[:, :, :n, :n], sp[:, :, n:, n:]], axis=2).reshape(b, 2 * hp, n, n)


def _pad_rows(x, multiple):
    t = x.shape[1]
    t_pad = -(-t // multiple) * multiple
    return x if t_pad == t else jnp.pad(x, ((0, 0), (0, t_pad - t), (0, 0)))


def kernel(x_prompt, x_sample, state_rwkv_wkv, state_rwkv_shift, state_lru_h, state_lru_conv,
           cache_sb_k, cache_sb_v, page_table, cache_mem_k, cache_mem_v, mem_prompt,
           g_mix, g_mem, g_memkv, g_ffn, wq_mem, wk_mem, wv_mem, wo_mem, qn_mem, kn_mem,
           w_ffn_gate, w_ffn_up, w_ffn_down, w_in_ab, mu_shift, rw_w0, rw_w2, rw_a0, rw_a2,
           rw_g2, rw_kk, rw_ka, rw_rk, rw_lnx_w, rw_lnx_b, lru_conv_w, lru_conv_b, lru_wa,
           lru_ba, lru_wx, lru_bx, lru_lambda, w_out_ab, w_qkv_sb, w_out_sb, sb_bias):
    depth, d_model = g_mix.shape
    mem_heads = cache_mem_k.shape[3]
    mem_w = wq_mem.shape[2]
    rw_w = rw_w0.shape[1]
    rw_cols = mu_shift.shape[1]
    lru_w = lru_lambda.shape[1]
    rw_heads = rw_w // RW_HD
    sb_heads = sb_bias.shape[1]
    sb_w = sb_heads * SB_HD
    d_ff = w_ffn_gate.shape[2]
    lora_w = rw_w2.shape[1]
    lora_a = rw_a2.shape[1]
    assert lora_w + lora_a == LANES and rw_g2.shape[1] == LANES
    tf = 256
    assert d_ff % tf == 0
    bf = lambda z: z.astype(BF16)

    wg3 = [bf(w_ffn_gate[i]).reshape(d_model, d_ff // tf, tf).transpose(1, 0, 2) for i in range(depth)]
    wu3 = [bf(w_ffn_up[i]).reshape(d_model, d_ff // tf, tf).transpose(1, 0, 2) for i in range(depth)]
    wd3 = [bf(w_ffn_down[i]).reshape(d_ff // tf, tf, d_model) for i in range(depth)]
    head_id = jnp.arange(LANES) // RW_HD
    bd = (head_id[:, None] == head_id[None, :]).astype(BF16)

    def even_params(e):
        zw = jnp.zeros((lora_a, rw_w), F32)
        za = jnp.zeros((lora_w, rw_w), F32)
        row = lambda z: z.reshape(1, -1)
        rw = dict(mu=row(mu_shift[e]), w0=row(rw_w0[e]), w2=bf(jnp.concatenate([rw_w2[e], zw], axis=0)),
                  a0=row(rw_a0[e]), a2=bf(jnp.concatenate([za, rw_a2[e]], axis=0)), g2=bf(rw_g2[e]),
                  kk=row(rw_kk[e]), ka=row(rw_ka[e]), rk=row(rw_rk[e]), lnw=row(rw_lnx_w[e]),
                  lnb=row(rw_lnx_b[e]), bd=bd)
        lru = dict(cw=lru_conv_w[e], cb=row(lru_conv_b[e]), wa=bf(_pair_blockdiag(lru_wa[e])),
                   ba=row(lru_ba[e]), wx=bf(_pair_blockdiag(lru_wx[e])), bx=row(lru_bx[e]),
                   lam=row(lru_lambda[e]))
        return rw, lru

    def trunk(x, sample):
        bsz, t, _ = x.shape
        m = bsz * t
        tm = 512 if m % 512 == 0 else m
        out = {}
        xf = x.reshape(m, d_model)
        for i in range(depth):
            if i % 2 == 0:
                e = i // 2
                rw, lru = even_params(e)
                p_rw, p_x, p_gate = _norm_proj(xf, g_mix[i], bf(w_in_ab[e]),
                                               [rw_cols, rw_cols + lru_w], tm)
                p_rw = p_rw.reshape(bsz, t, rw_cols)
                p_x = p_x.reshape(bsz, t, lru_w)
                p_gate = p_gate.reshape(bsz, t, lru_w)
                if sample:
                    wkv0, shift0 = state_rwkv_wkv[e], state_rwkv_shift[e]
                    h0, conv0 = state_lru_h[e], state_lru_conv[e]
                else:
                    wkv0 = jnp.zeros((bsz, rw_heads, RW_HD, RW_HD), F32)
                    shift0 = jnp.zeros((bsz, rw_cols), F32)
                    h0 = jnp.zeros((bsz, lru_w), F32)
                    conv0 = jnp.zeros((bsz, CONV_W - 1, lru_w), F32)
                o_rw, s_pairs = _rwkv_mix(_pad_rows(p_rw, RW_CHUNK), shift0.reshape(bsz, 1, rw_cols),
                                          _state_to_pairs(wkv0), rw, RW_CHUNK, t)
                conv0_pad = jnp.pad(conv0, ((0, 0), (SUBLANES - (CONV_W - 1), 0), (0, 0)))
                px_pad = _pad_rows(p_x, BF16_ROWS)
                c_lru = LRU_CHUNK if px_pad.shape[1] % LRU_CHUNK == 0 else px_pad.shape[1]
                o_lru, h_last = _lru_mix(px_pad, _pad_rows(p_gate, BF16_ROWS), conv0_pad,
                                         h0.reshape(bsz, 1, lru_w), lru, c_lru, t)
                out.setdefault("wkv", []).append(_pairs_to_state(s_pairs, RW_HD))
                out.setdefault("shift", []).append(p_rw[:, t - 1])
                out.setdefault("lru_h", []).append(h_last[:, 0])
                out.setdefault("lru_conv", []).append(
                    jnp.concatenate([conv0, p_x], axis=1)[:, -(CONV_W - 1):])
                w_out = bf(w_out_ab[e])
                mix_in, mix_w = [o_rw[:, :t], o_lru[:, :t]], [w_out[:rw_w], w_out[rw_w:]]
            else:
                o = i // 2
                (qkv,) = _norm_proj(xf, g_mix[i], bf(w_qkv_sb[o]), [], tm)
                qkv = qkv.reshape(bsz, t, 3 * sb_w)
                k, v = qkv[:, :, sb_w:2 * sb_w], qkv[:, :, 2 * sb_w:]
                if sample:
                    pool = cache_sb_k.shape[1]
                    k_t = jnp.transpose(cache_sb_k, (0, 1, 3, 4, 2)).reshape(-1, PAGE)
                    v_t = jnp.transpose(cache_sb_v, (0, 1, 3, 4, 2)).reshape(-1, PAGE)
                    att = _sb_decode(qkv[:, :, :sb_w], k_t, v_t, page_table + o * pool, sb_bias[o],
                                     sb_heads, DECODE_PAGES_PER_STEP)
                else:
                    att = _sb_prompt(qkv, sb_bias[o], sb_heads)
                out.setdefault("sb_k", []).append(k.reshape(bsz, t, sb_heads, SB_HD))
                out.setdefault("sb_v", []).append(v.reshape(bsz, t, sb_heads, SB_HD))
                mix_in, mix_w = [att.reshape(bsz, t, sb_w)], [bf(w_out_sb[o])]
            if sample:
                mk = cache_mem_k[i].reshape(bsz, -1, mem_w)
                mv = cache_mem_v[i].reshape(bsz, -1, mem_w)
            else:
                mk, mv = _memory_kv(mem_prompt, g_memkv[i], bf(wk_mem[i]), bf(wv_mem[i]), kn_mem[i],
                                    mem_heads)
                out.setdefault("mem_k", []).append(mk.reshape(bsz, -1, mem_heads, mem_w // mem_heads))
                out.setdefault("mem_v", []).append(mv.reshape(bsz, -1, mem_heads, mem_w // mem_heads))
            x3 = _pad_rows(xf.reshape(bsz, t, d_model), BF16_ROWS)
            t3 = x3.shape[1]
            big = t3 % 512 == 0
            x3 = _memx_residual(x3, [_pad_rows(a, BF16_ROWS) for a in mix_in], mix_w, g_mem[i],
                                bf(wq_mem[i]), qn_mem[i], mk, mv, bf(wo_mem[i]), mem_heads,
                                512 if big else t3, 1 if big or bsz % MEMX_ROWS_PER_STEP else MEMX_ROWS_PER_STEP)
            xf = x3[:, :t].reshape(m, d_model)
            xf = _ffn_residual(xf, g_ffn[i], wg3[i], wu3[i], wd3[i], tm)
        return xf.reshape(bsz, t, d_model), out

    y_p, sp = trunk(x_prompt, False)
    y_s, ss = trunk(x_sample, True)
    st = lambda arrs: jnp.stack(arrs, axis=0)
    return (y_p, y_s, st(sp["wkv"]), st(ss["wkv"]), st(sp["shift"]), st(ss["shift"]),
            st(sp["lru_h"]), st(ss["lru_h"]), st(sp["lru_conv"]), st(ss["lru_conv"]),
            st(sp["sb_k"]), st(ss["sb_k"]), st(sp["sb_v"]), st(ss["sb_v"]),
            st(sp["mem_k"]), st(sp["mem_v"]))
```
